```python
import math
import jax, jax.numpy as jnp
from jax import lax
import numpy as np

D_MODEL = 2048
BATCH = 2
SEQ = 8192
DEPTH = 1

HG_HEADS = 8
HG_DK = 128
HG_DV = 128
HG_KEY = HG_HEADS * HG_DK
HG_VAL = HG_HEADS * HG_DV
HG_CHUNK = 64
SSD_HEADS = 16
SSD_HEADDIM = 64
SSD_WIDTH = SSD_HEADS * SSD_HEADDIM
SSD_GROUPS = 2
SSD_HPG = SSD_HEADS // SSD_GROUPS
SSD_STATE = 128
SSD_CONV = 4
SSD_CHUNK = 128
SSD_CONV_DIM = SSD_WIDTH + 2 * SSD_GROUPS * SSD_STATE
D_MIX = HG_VAL + SSD_WIDTH
D_FF = ((8 * D_MODEL + 3 * 256 - 1) // (3 * 256)) * 256
NORM_EPS = 1e-6
IN_SPLITS = (HG_KEY, HG_KEY, HG_VAL, HG_VAL, SSD_WIDTH, SSD_CONV_DIM, SSD_HEADS)
N_IN = HG_KEY * 2 + HG_VAL * 2 + SSD_WIDTH + SSD_CONV_DIM + SSD_HEADS

kernel_name = "hgrn2_mamba2_parallel_hybrid_block"


def _split_points():
    pts, acc = [], 0
    for s in IN_SPLITS[:-1]:
        acc += s
        pts.append(acc)
    return pts


def rms_norm(x, w):
    xf = x.astype(jnp.float32)
    y = xf * lax.rsqrt(jnp.mean(xf * xf, axis=-1, keepdims=True) + NORM_EPS)
    return (y * w.astype(jnp.float32)).astype(x.dtype)


def group_rms_norm(x, w, n_groups):
    xf = x.astype(jnp.float32)
    shp = xf.shape
    xg = xf.reshape(shp[:-1] + (n_groups, shp[-1] // n_groups))
    xg = xg * lax.rsqrt(jnp.mean(xg * xg, axis=-1, keepdims=True) + NORM_EPS)
    return xg.reshape(shp) * w.astype(jnp.float32)


def hgrn2_mixer(q_raw, f_raw, i_in, g_in, lb, norm_w):
    bsz, seqlen, _ = q_raw.shape
    nc = seqlen // HG_CHUNK
    q = jax.nn.silu(q_raw.astype(jnp.float32)) * (HG_DK ** -0.5)
    forget = lb + (1.0 - lb) * jax.nn.sigmoid(f_raw.astype(jnp.float32))
    log_f = jnp.log(forget)
    k = 1.0 - forget
    v = i_in.astype(jnp.float32)

    def to_chunks(t, d):
        return t.reshape(bsz, nc, HG_CHUNK, HG_HEADS, d).transpose(1, 0, 3, 2, 4)

    causal = jnp.tril(jnp.ones((HG_CHUNK, HG_CHUNK), dtype=bool))

    def step(S, inp):
        qc, kc, vc, gc = inp
        b = jnp.cumsum(gc, axis=2)
        rel = jnp.where(causal[:, :, None], b[:, :, :, None, :] - b[:, :, None, :, :], -jnp.inf)
        scores = jnp.sum(qc[:, :, :, None, :] * kc[:, :, None, :, :] * jnp.exp(rel), axis=-1)
        o = (jnp.einsum('bhts,bhsv->bhtv', scores, vc)
             + jnp.einsum('bhtk,bhkv->bhtv', qc * jnp.exp(b), S))
        b_last = b[:, :, -1, :]
        S_new = (S * jnp.exp(b_last)[..., None]
                 + jnp.einsum('bhsk,bhsv->bhkv', kc * jnp.exp(b_last[:, :, None, :] - b), vc))
        return S_new, o

    S0 = jnp.zeros((bsz, HG_HEADS, HG_DK, HG_DV), jnp.float32)
    _, o = lax.scan(step, S0, (to_chunks(q, HG_DK), to_chunks(k, HG_DK),
                               to_chunks(v, HG_DV), to_chunks(log_f, HG_DK)))
    o = o.transpose(1, 0, 3, 2, 4).reshape(bsz, seqlen, HG_VAL)
    o = group_rms_norm(o, norm_w, HG_HEADS) * jax.nn.silu(g_in.astype(jnp.float32))
    return o.astype(q_raw.dtype)


def causal_depthwise_conv(u, w, b):
    ch = u.shape[-1]
    out = lax.conv_general_dilated(u, w[:, None, :].astype(u.dtype), window_strides=(1,),
                                   padding=[(SSD_CONV - 1, 0)],
                                   dimension_numbers=('NWC', 'WIO', 'NWC'),
                                   feature_group_count=ch)
    return out + b.astype(u.dtype)


def ssd_chunked(x, a, Bm, Cm):
    bsz, seqlen = x.shape[:2]
    nc = seqlen // SSD_CHUNK
    x = x.reshape(bsz, nc, SSD_CHUNK, SSD_GROUPS, SSD_HPG, SSD_HEADDIM)
    a = a.reshape(bsz, nc, SSD_CHUNK, SSD_GROUPS, SSD_HPG)
    Bm = Bm.reshape(bsz, nc, SSD_CHUNK, SSD_GROUPS, SSD_STATE)
    Cm = Cm.reshape(bsz, nc, SSD_CHUNK, SSD_GROUPS, SSD_STATE)
    a_cs = jnp.cumsum(a, axis=2)
    tril = jnp.tril(jnp.ones((SSD_CHUNK, SSD_CHUNK), dtype=bool))
    seg = jnp.where(tril[:, :, None, None], a_cs[:, :, :, None] - a_cs[:, :, None, :], -jnp.inf)
    decay_in = jnp.exp(seg)
    cb = jnp.einsum('bctgn,bcsgn->bctsg', Cm, Bm)
    y_diag = jnp.einsum('bctsgh,bcsghp->bctghp', cb[..., None] * decay_in, x)
    decay_states = jnp.exp(a_cs[:, :, -1:] - a_cs)
    states = jnp.einsum('bcsgn,bcsghp->bcghpn', Bm, x * decay_states[..., None])
    states = jnp.concatenate([jnp.zeros_like(states[:, :1]), states], axis=1)
    chunk_tot = jnp.pad(a_cs[:, :, -1], ((0, 0), (1, 0), (0, 0), (0, 0)))
    cs2 = jnp.cumsum(chunk_tot, axis=1)
    trilc = jnp.tril(jnp.ones((nc + 1, nc + 1), dtype=bool))
    seg_c = jnp.where(trilc[:, :, None, None], cs2[:, :, None] - cs2[:, None, :], -jnp.inf)
    new_states = jnp.einsum('bzcgh,bcghpn->bzghpn', jnp.exp(seg_c), states)
    prev = new_states[:, :-1]
    y_off = jnp.einsum('bctgn,bcghpn->bctghp', Cm, prev) * jnp.exp(a_cs)[..., None]
    return (y_diag + y_off).reshape(bsz, seqlen, SSD_GROUPS, SSD_HPG, SSD_HEADDIM)


def ssd_mixer(z, xbc, dt_raw, conv_w, conv_b, dt_bias, a_log, d_skip, norm_w):
    bsz, seqlen, _ = z.shape
    xbc = jax.nn.silu(causal_depthwise_conv(xbc, conv_w, conv_b)).astype(jnp.float32)
    xs = xbc[..., :SSD_WIDTH].reshape(bsz, seqlen, SSD_GROUPS, SSD_HPG, SSD_HEADDIM)
    Bm = xbc[..., SSD_WIDTH:SSD_WIDTH + SSD_GROUPS * SSD_STATE].reshape(bsz, seqlen, SSD_GROUPS, SSD_STATE)
    Cm = xbc[..., SSD_WIDTH + SSD_GROUPS * SSD_STATE:].reshape(bsz, seqlen, SSD_GROUPS, SSD_STATE)
    dt = jax.nn.softplus(dt_raw.astype(jnp.float32) + dt_bias.astype(jnp.float32))
    dt = dt.reshape(bsz, seqlen, SSD_GROUPS, SSD_HPG)
    A = -jnp.exp(a_log.astype(jnp.float32)).reshape(SSD_GROUPS, SSD_HPG)
    y = ssd_chunked(xs * dt[..., None], dt * A, Bm, Cm)
    y = y + d_skip.astype(jnp.float32).reshape(SSD_GROUPS, SSD_HPG)[:, :, None] * xs
    y = y.reshape(bsz, seqlen, SSD_WIDTH)
    y = group_rms_norm(y * jax.nn.silu(z.astype(jnp.float32)), norm_w, SSD_GROUPS)
    return y.astype(z.dtype)


def setup_inputs(seed: int = 0) -> dict:
    key = jax.random.key(seed)
    ks = jax.random.split(key, 20)
    f32 = jnp.float32

    def nrm(k, shape, scale):
        return jax.random.normal(k, shape, f32) * scale

    x = nrm(ks[0], (BATCH, SEQ, D_MODEL), 1.0)
    pre_mix_norm_w = 1.0 + nrm(ks[1], (DEPTH, D_MODEL), 0.02)
    w_in = nrm(ks[2], (DEPTH, D_MODEL, N_IN), D_MODEL ** -0.5)
    lb_logits = nrm(ks[3], (DEPTH + 1, HG_KEY), 0.5)
    conv_w = nrm(ks[4], (DEPTH, SSD_CONV, SSD_CONV_DIM), SSD_CONV ** -0.5)
    conv_b = nrm(ks[5], (DEPTH, SSD_CONV_DIM), 0.02)
    dt0 = jnp.exp(jax.random.uniform(ks[6], (DEPTH, SSD_HEADS), f32, math.log(1e-3), math.log(1e-1)))
    dt_bias = dt0 + jnp.log(-jnp.expm1(-dt0))
    a_log = jnp.log(jax.random.uniform(ks[7], (DEPTH, SSD_HEADS), f32, 1.0, 16.0))
    d_skip = 1.0 + nrm(ks[8], (DEPTH, SSD_HEADS), 0.1)
    hgrn_norm_w = 1.0 + nrm(ks[9], (DEPTH, HG_VAL), 0.02)
    ssd_norm_w = 1.0 + nrm(ks[10], (DEPTH, SSD_WIDTH), 0.02)
    w_out = nrm(ks[11], (DEPTH, D_MIX, D_MODEL), D_MIX ** -0.5)
    post_mix_norm_w = 1.0 + nrm(ks[12], (DEPTH, D_MODEL), 0.02)
    pre_ffn_norm_w = 1.0 + nrm(ks[13], (DEPTH, D_MODEL), 0.02)
    w_gate = nrm(ks[14], (DEPTH, D_MODEL, D_FF), D_MODEL ** -0.5)
    w_up = nrm(ks[15], (DEPTH, D_MODEL, D_FF), D_MODEL ** -0.5)
    w_down = nrm(ks[16], (DEPTH, D_FF, D_MODEL), D_FF ** -0.5)
    post_ffn_norm_w = 1.0 + nrm(ks[17], (DEPTH, D_MODEL), 0.02)
    return {"x": x, "pre_mix_norm_w": pre_mix_norm_w, "w_in": w_in, "lb_logits": lb_logits,
            "conv_w": conv_w, "conv_b": conv_b, "dt_bias": dt_bias, "a_log": a_log,
            "d_skip": d_skip, "hgrn_norm_w": hgrn_norm_w, "ssd_norm_w": ssd_norm_w,
            "w_out": w_out, "post_mix_norm_w": post_mix_norm_w, "pre_ffn_norm_w": pre_ffn_norm_w,
            "w_gate": w_gate, "w_up": w_up, "w_down": w_down, "post_ffn_norm_w": post_ffn_norm_w}


def reference(x, pre_mix_norm_w, w_in, lb_logits, conv_w, conv_b, dt_bias, a_log, d_skip,
              hgrn_norm_w, ssd_norm_w, w_out, post_mix_norm_w, pre_ffn_norm_w,
              w_gate, w_up, w_down, post_ffn_norm_w):
    lb_all = jnp.cumsum(jax.nn.softmax(lb_logits.astype(jnp.float32), axis=0), axis=0)
    split_pts = _split_points()
    for l in range(DEPTH):
        h = rms_norm(x, pre_mix_norm_w[l])
        proj = jnp.einsum('bld,dn->bln', h, w_in[l])
        q_raw, f_raw, i_in, g_in, z, xbc, dt_raw = jnp.split(proj, split_pts, axis=-1)
        o_a = hgrn2_mixer(q_raw, f_raw, i_in, g_in, lb_all[l], hgrn_norm_w[l])
        o_b = ssd_mixer(z, xbc, dt_raw, conv_w[l], conv_b[l], dt_bias[l], a_log[l],
                        d_skip[l], ssd_norm_w[l])
        mix = jnp.einsum('blm,md->bld', jnp.concatenate([o_a, o_b], axis=-1), w_out[l])
        x = x + rms_norm(mix, post_mix_norm_w[l])
        h = rms_norm(x, pre_ffn_norm_w[l])
        hid = jax.nn.silu(jnp.einsum('bld,df->blf', h, w_gate[l])) * jnp.einsum('bld,df->blf', h, w_up[l])
        ff = jnp.einsum('blf,fd->bld', hid, w_down[l])
        x = x + rms_norm(ff, post_ffn_norm_w[l])
    return x
```

```python
import functools

import jax
import jax.numpy as jnp
from jax import lax
from jax.experimental import pallas as pl
from jax.experimental.pallas import tpu as pltpu

F32 = jnp.float32
BF16 = jnp.bfloat16

HG_HEADS = 8
HG_DK = 128
HG_DV = 128
HG_KEY = HG_HEADS * HG_DK
HG_VAL = HG_HEADS * HG_DV
SSD_HEADS = 16
SSD_HEADDIM = 64
SSD_WIDTH = SSD_HEADS * SSD_HEADDIM
SSD_GROUPS = 2
SSD_HPG = SSD_HEADS // SSD_GROUPS
SSD_STATE = 128
SSD_CONV = 4
SSD_GROUP_WIDTH = SSD_HPG * SSD_HEADDIM
SSD_BC_WIDTH = 2 * SSD_GROUPS * SSD_STATE
NORM_EPS = 1e-6

V7X_LANES = 128
V7X_SUBLANES = 8
V7X_VMEM_BYTES = 64 * 1024 * 1024
V7X_VMEM_COMPILER_RESERVE = 8 * 1024 * 1024

PROJ_DTYPE = F32
MIX_DTYPE = BF16
INPROJ_TM = 1024
INPROJ_TN = 512
HGRN_WINDOW = 256
SSD_CHUNK = 256
OUTPROJ_TM = 512
FFN_TM = 512
FFN_TF = 512
NEG_BIG = -1e30


def _nbytes(shape, dtype):
    n = 1
    for s in shape:
        n *= s
    return n * jnp.dtype(dtype).itemsize


def _vmem_limit(block_bytes, scratch_bytes=0):
    need = 2 * block_bytes + scratch_bytes + V7X_VMEM_COMPILER_RESERVE
    return int(min(need, V7X_VMEM_BYTES - 4 * 1024 * 1024))


def _sigmoid(x):
    return 1.0 / (1.0 + jnp.exp(-x))


def _silu(x):
    return x * _sigmoid(x)


def _rms(x):
    return x * lax.rsqrt(jnp.mean(x * x, axis=-1, keepdims=True) + NORM_EPS)


def _dot(a, b):
    return jnp.dot(a, b, preferred_element_type=F32)


def _dot_nt(a, b):
    return lax.dot_general(a, b, (((1,), (1,)), ((), ())), preferred_element_type=F32)


def _inproj_body(x_ref, nw_ref, w_ref, wdt_ref, o_ref, odt_ref, h_ref):
    @pl.when(pl.program_id(1) == 0)
    def _():
        hb = (_rms(x_ref[...]) * nw_ref[...]).astype(BF16)
        h_ref[...] = hb
        odt_ref[...] = _dot(hb, wdt_ref[...])

    o_ref[...] = _dot(h_ref[...], w_ref[...]).astype(o_ref.dtype)


def _in_proj(x2, norm_w, w_main, w_dt):
    t, d = x2.shape
    n = w_main.shape[1]
    tm, tn = INPROJ_TM, INPROJ_TN
    blocks = (_nbytes((tm, d), F32) + _nbytes((d, tn), BF16) + _nbytes((d, V7X_LANES), BF16)
              + _nbytes((tm, tn), PROJ_DTYPE) + _nbytes((tm, V7X_LANES), F32))
    return pl.pallas_call(
        _inproj_body,
        grid=(t // tm, n // tn),
        in_specs=[
            pl.BlockSpec((tm, d), lambda i, j: (i, 0)),
            pl.BlockSpec((1, d), lambda i, j: (0, 0)),
            pl.BlockSpec((d, tn), lambda i, j: (0, j)),
            pl.BlockSpec((d, V7X_LANES), lambda i, j: (0, 0)),
        ],
        out_specs=[
            pl.BlockSpec((tm, tn), lambda i, j: (i, j)),
            pl.BlockSpec((tm, V7X_LANES), lambda i, j: (i, 0)),
        ],
        out_shape=[
            jax.ShapeDtypeStruct((t, n), PROJ_DTYPE),
            jax.ShapeDtypeStruct((t, V7X_LANES), F32),
        ],
        scratch_shapes=[pltpu.VMEM((tm, d), BF16)],
        compiler_params=pltpu.CompilerParams(
            dimension_semantics=("parallel", "arbitrary"),
            vmem_limit_bytes=_vmem_limit(blocks, _nbytes((tm, d), BF16)),
        ),
        name="in_proj",
    )(x2, norm_w, w_main, w_dt)


def _hgrn_levels(w):
    ps = []
    p = 1
    while p < w:
        ps.append(p)
        p *= 2
    return ps


def _hgrn_body(q_ref, f_ref, i_ref, g_ref, lbl_ref, nw_ref, o_ref, st_ref, lvl_ref, lb_ref,
               *, layer):
    w = HGRN_WINDOW

    @pl.when(pl.program_id(1) == 0)
    def _():
        st_ref[...] = jnp.zeros_like(st_ref)
        t_idx = lax.broadcasted_iota(jnp.int32, (w, w), 0)
        s_idx = lax.broadcasted_iota(jnp.int32, (w, w), 1)
        lv = 32 - lax.clz(t_idx ^ s_idx)
        lvl_ref[...] = jnp.where(t_idx >= s_idx, lv, -1)
        lg = lbl_ref[...]
        e = jnp.exp(lg - jnp.max(lg, axis=0, keepdims=True))
        sm = e / jnp.sum(e, axis=0, keepdims=True)
        lb_ref[...] = jnp.sum(sm[: layer + 1], axis=0, keepdims=True)

    row = lax.broadcasted_iota(jnp.int32, (w, HG_DK), 0)
    shp3 = (w // V7X_SUBLANES, V7X_SUBLANES, HG_DK)

    def head(h, carry):
        ks = pl.ds(pl.multiple_of(h * HG_DK, HG_DK), HG_DK)
        vs = pl.ds(pl.multiple_of(h * HG_DV, HG_DV), HG_DV)
        q_raw = q_ref[:, ks].astype(F32)
        f_raw = f_ref[:, ks].astype(F32)
        v = i_ref[:, vs].astype(F32)
        gate = g_ref[:, vs].astype(F32)
        lb = lb_ref[:, ks]

        q = _silu(q_raw) * (HG_DK ** -0.5)
        f = lb + (1.0 - lb) * _sigmoid(f_raw)
        k = 1.0 - f

        a = jnp.where(lvl_ref[...] == 0, _dot_nt(q.astype(BF16), k.astype(BF16)), 0.0)

        cq = f
        ck = jnp.ones_like(f)
        r = f
        for li, p in enumerate(_hgrn_levels(w)):
            s_p = _dot_nt((q * cq).astype(BF16), (k * ck).astype(BF16))
            a = jnp.where(lvl_ref[...] == li + 1, s_p, a)
            if p < V7X_SUBLANES:
                r3 = r.reshape(shp3)
                down = pltpu.roll(r3, p, axis=1).reshape(w, HG_DK)
                up = pltpu.roll(r3, V7X_SUBLANES - p, axis=1).reshape(w, HG_DK)
            else:
                down = pltpu.roll(r, p, axis=0)
                up = pltpu.roll(r, w - p, axis=0)
            upper = (row & p) != 0
            cq = cq * jnp.where(upper, down, 1.0)
            ck = ck * jnp.where(upper, 1.0, up)
            r = r * jnp.where(upper, down, up)

        st = st_ref[h]
        o = _dot(a.astype(BF16), v.astype(BF16)) + _dot_nt((q * cq).astype(BF16), st.astype(BF16))
        st_ref[h] = st * r[0:1, :] + _dot(v.T.astype(BF16), (k * ck).astype(BF16))

        on = _rms(o) * nw_ref[:, vs]
        o_ref[:, vs] = (on * _silu(gate)).astype(o_ref.dtype)
        return carry

    lax.fori_loop(0, HG_HEADS, head, 0)


def _hgrn2(proj, lb_logits, norm_w, batch, seqlen, layer):
    w = HGRN_WINDOW
    nw = seqlen // w
    blocks = 4 * _nbytes((w, HG_KEY), proj.dtype) + _nbytes((w, HG_VAL), MIX_DTYPE)
    scratch = (_nbytes((HG_HEADS, HG_DV, HG_DK), F32) + _nbytes((w, w), jnp.int32)
               + _nbytes((V7X_SUBLANES, HG_KEY), F32))

    def col(c):
        return pl.BlockSpec((w, HG_KEY), lambda b, i: (b * nw + i, c))

    return pl.pallas_call(
        functools.partial(_hgrn_body, layer=layer),
        grid=(batch, nw),
        in_specs=[
            col(0), col(1), col(2), col(3),
            pl.BlockSpec(lb_logits.shape, lambda b, i: (0, 0)),
            pl.BlockSpec((1, HG_VAL), lambda b, i: (0, 0)),
        ],
        out_specs=pl.BlockSpec((w, HG_VAL), lambda b, i: (b * nw + i, 0)),
        out_shape=jax.ShapeDtypeStruct((batch * seqlen, HG_VAL), MIX_DTYPE),
        scratch_shapes=[
            pltpu.VMEM((HG_HEADS, HG_DV, HG_DK), F32),
            pltpu.VMEM((w, w), jnp.int32),
            pltpu.VMEM((1, HG_KEY), F32),
        ],
        compiler_params=pltpu.CompilerParams(
            dimension_semantics=("parallel", "arbitrary"),
            vmem_limit_bytes=_vmem_limit(blocks, scratch),
        ),
        name="hgrn2",
    )(proj, proj, proj, proj, lb_logits, norm_w)


def _split3(x):
    hi = x.astype(BF16)
    r1 = x - hi.astype(F32)
    mid = r1.astype(BF16)
    lo = (r1 - mid.astype(F32)).astype(BF16)
    return hi, mid, lo


def _dot_exact01(x, m01):
    hi, mid, lo = _split3(x)
    return _dot(hi, m01) + _dot(mid, m01) + _dot(lo, m01)


def _ssd_body(z_ref, x0_ref, x1_ref, bc_ref, dt_ref, cw_ref, cb_ref, dtb_ref, alog_ref,
              dskip_ref, nw_ref, expand_ref, o_ref, xbuf_ref, state_ref):
    wc = SSD_CHUNK
    hdr = V7X_SUBLANES
    gw = SSD_GROUP_WIDTH

    @pl.when(pl.program_id(1) == 0)
    def _():
        state_ref[...] = jnp.zeros_like(state_ref)
        xbuf_ref[:, 0:hdr, :] = jnp.zeros((3, hdr, gw), F32)

    conv = []
    for pi, ref in enumerate((x0_ref, x1_ref, bc_ref)):
        xbuf_ref[pi, hdr:hdr + wc, :] = ref[...].astype(F32)
        cs = slice(pi * gw, (pi + 1) * gw)
        acc = cb_ref[:, cs]
        for kk in range(SSD_CONV):
            off = hdr - (SSD_CONV - 1) + kk
            acc = acc + cw_ref[kk:kk + 1, cs] * xbuf_ref[pi, off:off + wc, :]
        xbuf_ref[pi, 0:hdr, :] = xbuf_ref[pi, wc:wc + hdr, :]
        conv.append(_silu(acc))
    xs = conv[0:2]
    bcs = conv[2]

    dtr = dt_ref[...] + dtb_ref[...]
    dt = jnp.maximum(dtr, 0.0) + jnp.log1p(jnp.exp(-jnp.abs(dtr)))
    a = dt * (-jnp.exp(alog_ref[...]))
    t_idx = lax.broadcasted_iota(jnp.int32, (wc, wc), 0)
    s_idx = lax.broadcasted_iota(jnp.int32, (wc, wc), 1)
    causal = t_idx >= s_idx
    tril01 = jnp.where(causal, 1.0, 0.0).astype(BF16)
    acs = _dot_exact01_left(tril01, a)
    acs_t = acs.T

    expand = expand_ref[...]
    acs_x = _dot_exact01(acs, expand)
    dt_x = _dot_exact01(dt, expand)
    lane = lax.broadcasted_iota(jnp.int32, (wc, V7X_LANES), 1)
    first_half = lane < SSD_HEADDIM

    for g in range(SSD_GROUPS):
        gs = slice(g * gw, (g + 1) * gw)
        xs_g = xs[g]
        xdt = xs_g * dt_x[:, gs]
        xdt_b = xdt.astype(BF16)
        b_g = bcs[:, g * SSD_STATE:(g + 1) * SSD_STATE]
        c_g = bcs[:, (SSD_GROUPS + g) * SSD_STATE:(SSD_GROUPS + g + 1) * SSD_STATE]
        c_b = c_g.astype(BF16)
        cb = _dot_nt(c_b, b_g.astype(BF16))

        pieces = []
        for j in range(SSD_HPG // 2):
            xp = xdt_b[:, j * V7X_LANES:(j + 1) * V7X_LANES]
            acc = None
            for half in range(2):
                h = g * SSD_HPG + 2 * j + half
                seg = acs[:, h:h + 1] - acs_t[h:h + 1, :]
                m = (cb * jnp.exp(jnp.where(causal, seg, NEG_BIG))).astype(BF16)
                keep = first_half if half == 0 else jnp.logical_not(first_half)
                part = _dot(m, jnp.where(keep, xp, jnp.zeros_like(xp)))
                acc = part if acc is None else acc + part
            pieces.append(acc)
        y_diag = jnp.concatenate(pieces, axis=-1)

        st = state_ref[g]
        acs_g = acs_x[:, gs]
        y_off = _dot(c_b, st.astype(BF16)) * jnp.exp(acs_g)
        y = y_diag + y_off + dskip_ref[:, gs] * xs_g

        last = acs_g[wc - 1:wc, :]
        xdec = (xdt * jnp.exp(last - acs_g)).astype(BF16)
        state_ref[g] = st * jnp.exp(last) + _dot(b_g.T.astype(BF16), xdec)

        yz = y * _silu(z_ref[:, gs].astype(F32))
        o_ref[:, gs] = (_rms(yz) * nw_ref[:, gs]).astype(o_ref.dtype)


def _dot_exact01_left(m01, x):
    hi, mid, lo = _split3(x)
    return _dot(m01, hi) + _dot(m01, mid) + _dot(m01, lo)


def _ssd(proj, dt_raw, conv_w, conv_b, dt_bias_p, a_log_p, d_skip_x, norm_w, expand, batch, seqlen):
    wc = SSD_CHUNK
    nc = seqlen // wc
    gw = SSD_GROUP_WIDTH
    z_blk = (HG_KEY * 2 + HG_VAL * 2) // SSD_WIDTH
    x_blk = (HG_KEY * 2 + HG_VAL * 2 + SSD_WIDTH) // gw
    hdr = V7X_SUBLANES
    blocks = (_nbytes((wc, SSD_WIDTH), proj.dtype) + 3 * _nbytes((wc, gw), proj.dtype)
              + _nbytes((wc, V7X_LANES), F32) + _nbytes((wc, SSD_WIDTH), MIX_DTYPE)
              + _nbytes((V7X_LANES, SSD_WIDTH), BF16))
    scratch = _nbytes((3, hdr + wc, gw), F32) + _nbytes((SSD_GROUPS, SSD_STATE, gw), F32)

    def tok(width, c):
        return pl.BlockSpec((wc, width), lambda b, i: (b * nc + i, c))

    def whole(arr):
        return pl.BlockSpec(arr.shape, lambda b, i: (0,) * arr.ndim)

    return pl.pallas_call(
        _ssd_body,
        grid=(batch, nc),
        in_specs=[
            tok(SSD_WIDTH, z_blk), tok(gw, x_blk), tok(gw, x_blk + 1), tok(gw, x_blk + 2),
            tok(V7X_LANES, 0),
            whole(conv_w), whole(conv_b), whole(dt_bias_p), whole(a_log_p), whole(d_skip_x),
            whole(norm_w), whole(expand),
        ],
        out_specs=pl.BlockSpec((wc, SSD_WIDTH), lambda b, i: (b * nc + i, 0)),
        out_shape=jax.ShapeDtypeStruct((batch * seqlen, SSD_WIDTH), MIX_DTYPE),
        scratch_shapes=[
            pltpu.VMEM((3, hdr + wc, gw), F32),
            pltpu.VMEM((SSD_GROUPS, SSD_STATE, gw), F32),
        ],
        compiler_params=pltpu.CompilerParams(
            dimension_semantics=("parallel", "arbitrary"),
            vmem_limit_bytes=_vmem_limit(blocks, scratch),
        ),
        name="ssd",
    )(proj, proj, proj, proj, dt_raw, conv_w, conv_b, dt_bias_p, a_log_p, d_skip_x, norm_w, expand)


def _outproj_body(oa_ref, ob_ref, wa_ref, wb_ref, x_ref, nw_ref, x1_ref):
    mix = _dot(oa_ref[...], wa_ref[...]) + _dot(ob_ref[...], wb_ref[...])
    x1_ref[...] = x_ref[...] + _rms(mix) * nw_ref[...]


def _out_proj(o_a, o_b, w_a, w_b, x2, norm_w):
    t, d = x2.shape
    tm = OUTPROJ_TM
    blocks = (_nbytes((tm, HG_VAL), MIX_DTYPE) + _nbytes((tm, SSD_WIDTH), MIX_DTYPE)
              + _nbytes(w_a.shape, BF16) + _nbytes(w_b.shape, BF16) + 2 * _nbytes((tm, d), F32))
    return pl.pallas_call(
        _outproj_body,
        grid=(t // tm,),
        in_specs=[
            pl.BlockSpec((tm, HG_VAL), lambda i: (i, 0)),
            pl.BlockSpec((tm, SSD_WIDTH), lambda i: (i, 0)),
            pl.BlockSpec(w_a.shape, lambda i: (0, 0)),
            pl.BlockSpec(w_b.shape, lambda i: (0, 0)),
            pl.BlockSpec((tm, d), lambda i: (i, 0)),
            pl.BlockSpec((1, d), lambda i: (0, 0)),
        ],
        out_specs=pl.BlockSpec((tm, d), lambda i: (i, 0)),
        out_shape=jax.ShapeDtypeStruct((t, d), F32),
        compiler_params=pltpu.CompilerParams(
            dimension_semantics=("parallel",),
            vmem_limit_bytes=_vmem_limit(blocks),
        ),
        name="out_proj",
    )(o_a, o_b, w_a, w_b, x2, norm_w)


def _ffn_body(x1_ref, prew_ref, wg_ref, wu_ref, wd_ref, postw_ref, o_ref, h_ref):
    j = pl.program_id(1)

    @pl.when(j == 0)
    def _():
        h_ref[...] = (_rms(x1_ref[...]) * prew_ref[...]).astype(BF16)

    h = h_ref[...]
    hid = (_silu(_dot(h, wg_ref[...])) * _dot(h, wu_ref[...])).astype(BF16)
    part = _dot(hid, wd_ref[...])

    @pl.when(j == 0)
    def _():
        o_ref[...] = part

    @pl.when(j > 0)
    def _():
        o_ref[...] += part

    @pl.when(j == pl.num_programs(1) - 1)
    def _():
        o_ref[...] = x1_ref[...] + _rms(o_ref[...]) * postw_ref[...]


def _ffn(x1, pre_w, w_gate, w_up, w_down, post_w):
    t, d = x1.shape
    f = w_gate.shape[1]
    tm, tf = FFN_TM, FFN_TF
    blocks = (2 * _nbytes((tm, d), F32) + 2 * _nbytes((d, tf), BF16) + _nbytes((tf, d), BF16))
    return pl.pallas_call(
        _ffn_body,
        grid=(t // tm, f // tf),
        in_specs=[
            pl.BlockSpec((tm, d), lambda i, j: (i, 0)),
            pl.BlockSpec((1, d), lambda i, j: (0, 0)),
            pl.BlockSpec((d, tf), lambda i, j: (0, j)),
            pl.BlockSpec((d, tf), lambda i, j: (0, j)),
            pl.BlockSpec((tf, d), lambda i, j: (j, 0)),
            pl.BlockSpec((1, d), lambda i, j: (0, 0)),
        ],
        out_specs=pl.BlockSpec((tm, d), lambda i, j: (i, 0)),
        out_shape=jax.ShapeDtypeStruct((t, d), F32),
        scratch_shapes=[pltpu.VMEM((tm, d), BF16)],
        compiler_params=pltpu.CompilerParams(
            dimension_semantics=("parallel", "arbitrary"),
            vmem_limit_bytes=_vmem_limit(blocks, _nbytes((tm, d), BF16)),
        ),
        name="ffn",
    )(x1, pre_w, w_gate, w_up, w_down, post_w)


def _pad_lanes(v):
    return jnp.pad(v.astype(F32), (0, V7X_LANES - v.shape[0]))[None, :]


def kernel(x, pre_mix_norm_w, w_in, lb_logits, conv_w, conv_b, dt_bias, a_log, d_skip, hgrn_norm_w,
           ssd_norm_w, w_out, post_mix_norm_w, pre_ffn_norm_w, w_gate, w_up, w_down, post_ffn_norm_w):
    batch, seqlen, d = x.shape
    depth = w_in.shape[0]
    n_main = 2 * HG_KEY + 2 * HG_VAL + SSD_WIDTH + SSD_WIDTH + SSD_BC_WIDTH
    expand = (jnp.arange(V7X_LANES)[:, None] == (jnp.arange(SSD_WIDTH)[None, :] // SSD_HEADDIM)).astype(BF16)

    x2 = x.reshape(batch * seqlen, d)
    for l in range(depth):
        w_main = w_in[l, :, :n_main].astype(BF16)
        w_dt = jnp.pad(w_in[l, :, n_main:], ((0, 0), (0, V7X_LANES - SSD_HEADS))).astype(BF16)
        proj, dt_raw = _in_proj(x2, pre_mix_norm_w[l][None, :], w_main, w_dt)
        o_a = _hgrn2(proj, lb_logits.astype(F32), hgrn_norm_w[l][None, :], batch, seqlen, l)
        o_b = _ssd(proj, dt_raw, conv_w[l], conv_b[l][None, :], _pad_lanes(dt_bias[l]),
                   _pad_lanes(a_log[l]), jnp.repeat(d_skip[l].astype(F32), SSD_HEADDIM)[None, :],
                   ssd_norm_w[l][None, :], expand, batch, seqlen)
        x1 = _out_proj(o_a, o_b, w_out[l, :HG_VAL].astype(BF16), w_out[l, HG_VAL:].astype(BF16),
                       x2, post_mix_norm_w[l][None, :])
        x2 = _ffn(x1, pre_ffn_norm_w[l][None, :], w_gate[l].astype(BF16), w_up[l].astype(BF16),
                  w_down[l].astype(BF16), post_ffn_norm_w[l][None, :])
    return x2.reshape(batch, seqlen, d)
```

```python
import functools

import jax
import jax.numpy as jnp
from jax import lax
from jax.experimental import pallas as pl
from jax.experimental.pallas import tpu as pltpu

F32 = jnp.float32
BF16 = jnp.bfloat16

HG_HEADS = 8
HG_DK = 128
HG_DV = 128
HG_KEY = HG_HEADS * HG_DK
HG_VAL = HG_HEADS * HG_DV
SSD_HEADS = 16
SSD_HEADDIM = 64
SSD_WIDTH = SSD_HEADS * SSD_HEADDIM
SSD_GROUPS = 2
SSD_HPG = SSD_HEADS // SSD_GROUPS
SSD_STATE = 128
SSD_CONV = 4
SSD_GROUP_WIDTH = SSD_HPG * SSD_HEADDIM
SSD_BC_WIDTH = 2 * SSD_GROUPS * SSD_STATE
NORM_EPS = 1e-6

V7X_LANES = 128
V7X_SUBLANES = 8
V7X_VMEM_BYTES = 64 * 1024 * 1024
V7X_VMEM_COMPILER_RESERVE = 6 * 1024 * 1024
V7X_VMEM_UNSCOPED = 2 * 1024 * 1024

PROJ_DTYPE = F32
MIX_DTYPE = BF16
INPROJ_TM = 1024
INPROJ_TN = 512
HGRN_WINDOW = 256
HGRN_HALF = 128
HGRN_HEADS_PER_ITER = 4
SSD_CHUNK = 256
OUTPROJ_TM = 512
FFN_TM = 1024
FFN_TF = 512
FFN_SUB = 256
FFN_DOWN_COLS = 512
FFN_X1_COLS = 256
NEG_BIG = -1e30
LOG2_E = 1.4426950408889634


def _nbytes(shape, dtype):
    n = 1
    for s in shape:
        n *= s
    return n * jnp.dtype(dtype).itemsize


def _vmem_limit(block_bytes, scratch_bytes=0, value_bytes=0):
    need = 2 * block_bytes + scratch_bytes + value_bytes + V7X_VMEM_COMPILER_RESERVE
    return int(min(need, V7X_VMEM_BYTES - V7X_VMEM_UNSCOPED))


def _sigmoid(x):
    return 1.0 / (1.0 + jnp.exp2(x * (-LOG2_E)))


def _silu(x):
    return x * _sigmoid(x)


def _rms(x):
    return x * lax.rsqrt(jnp.mean(x * x, axis=-1, keepdims=True) + NORM_EPS)


def _dot(a, b):
    return jnp.dot(a, b, preferred_element_type=F32)


def _dot_nt(a, b):
    return lax.dot_general(a, b, (((1,), (1,)), ((), ())), preferred_element_type=F32)


def _inproj_body(x_ref, nw_ref, w_ref, wdt_ref, o_ref, odt_ref, h_ref):
    @pl.when(pl.program_id(1) == 0)
    def _():
        hb = (_rms(x_ref[...]) * nw_ref[...]).astype(BF16)
        h_ref[...] = hb
        odt_ref[...] = _dot(hb, wdt_ref[...])

    o_ref[...] = _dot(h_ref[...], w_ref[...]).astype(o_ref.dtype)


def _in_proj(x2, norm_w, w_main, w_dt):
    t, d = x2.shape
    n = w_main.shape[1]
    tm, tn = INPROJ_TM, INPROJ_TN
    blocks = (_nbytes((tm, d), F32) + _nbytes((d, tn), BF16) + _nbytes((d, V7X_LANES), BF16)
              + _nbytes((tm, tn), PROJ_DTYPE) + _nbytes((tm, V7X_LANES), F32))
    return pl.pallas_call(
        _inproj_body,
        grid=(t // tm, n // tn),
        in_specs=[
            pl.BlockSpec((tm, d), lambda i, j: (i, 0)),
            pl.BlockSpec((1, d), lambda i, j: (0, 0)),
            pl.BlockSpec((d, tn), lambda i, j: (0, j)),
            pl.BlockSpec((d, V7X_LANES), lambda i, j: (0, 0)),
        ],
        out_specs=[
            pl.BlockSpec((tm, tn), lambda i, j: (i, j)),
            pl.BlockSpec((tm, V7X_LANES), lambda i, j: (i, 0)),
        ],
        out_shape=[
            jax.ShapeDtypeStruct((t, n), PROJ_DTYPE),
            jax.ShapeDtypeStruct((t, V7X_LANES), F32),
        ],
        scratch_shapes=[pltpu.VMEM((tm, d), BF16)],
        compiler_params=pltpu.CompilerParams(
            dimension_semantics=("parallel", "arbitrary"),
            vmem_limit_bytes=_vmem_limit(blocks, _nbytes((tm, d), BF16)),
        ),
        name="in_proj",
    )(x2, norm_w, w_main, w_dt)


def _hgrn_levels(w):
    ps = []
    p = 1
    while p < w:
        ps.append(p)
        p *= 2
    return ps


def _hgrn_body(q_ref, f_ref, i_ref, g_ref, lbl_ref, nw_ref, o_ref, st_ref, lvl_ref, lb_ref,
               *, layer):
    w = HGRN_WINDOW
    hw = HGRN_HALF
    sub = V7X_SUBLANES
    n_half = w // hw
    assert n_half == 2 and hw == V7X_LANES

    @pl.when(pl.program_id(1) == 0)
    def _():
        st_ref[...] = jnp.zeros_like(st_ref)
        t_idx = lax.broadcasted_iota(jnp.int32, (hw, hw), 0)
        s_idx = lax.broadcasted_iota(jnp.int32, (hw, hw), 1)
        lv = 32 - lax.clz(t_idx ^ s_idx)
        lvl_ref[...] = jnp.where(t_idx >= s_idx, lv, -1)
        lg = lbl_ref[...]
        e = jnp.exp(lg - jnp.max(lg, axis=0, keepdims=True))
        sm = e / jnp.sum(e, axis=0, keepdims=True)
        lb_ref[...] = jnp.sum(sm[: layer + 1], axis=0, keepdims=True)

    sub_idx = lax.broadcasted_iota(jnp.int32, (1, sub, HG_DK), 1)
    shp3 = (w // sub, sub, HG_DK)

    def split2(x, p):
        x4 = x.reshape(x.shape[0] // (2 * p), 2, p, x.shape[1])
        return x4[:, 0], x4[:, 1]

    def merge2(lo, hi):
        return jnp.stack([lo, hi], axis=1).reshape(-1, lo.shape[-1])

    def head(h, carry):
        ks = pl.ds(pl.multiple_of(h * HG_DK, HG_DK), HG_DK)
        vs = pl.ds(pl.multiple_of(h * HG_DV, HG_DV), HG_DV)
        q_raw = q_ref[:, ks].astype(F32)
        f_raw = f_ref[:, ks].astype(F32)
        v = i_ref[:, vs].astype(F32)
        gate = g_ref[:, vs].astype(F32)
        lb = lb_ref[:, ks]

        q = _silu(q_raw) * (HG_DK ** -0.5)
        f = lb + (1.0 - lb) * _sigmoid(f_raw)
        k = 1.0 - f

        def half_scores(qn, kn, level, a_blocks):
            qb, kb = qn.astype(BF16), kn.astype(BF16)
            out = []
            for c in range(n_half):
                rs = slice(c * hw, (c + 1) * hw)
                s_c = _dot_nt(qb[rs], kb[rs])
                keep = lvl_ref[...] == level
                out.append(jnp.where(keep, s_c, 0.0 if a_blocks is None else a_blocks[c]))
            return out

        a_diag = half_scores(q, k, 0, None)
        cq = f.reshape(shp3)
        ck = jnp.ones(shp3, F32)
        r = cq
        q3, k3 = q.reshape(shp3), k.reshape(shp3)
        level = 1
        p = 1
        while p < sub:
            a_diag = half_scores((q3 * cq).reshape(w, HG_DK), (k3 * ck).reshape(w, HG_DK), level, a_diag)
            upper = (sub_idx & p) != 0
            down = pltpu.roll(r, p, axis=1)
            up = pltpu.roll(r, sub - p, axis=1)
            cq = cq * jnp.where(upper, down, 1.0)
            ck = ck * jnp.where(upper, 1.0, up)
            r = r * jnp.where(upper, down, up)
            p *= 2
            level += 1
        cq = cq.reshape(w, HG_DK)
        ck = ck.reshape(w, HG_DK)

        while p < hw:
            cq_lo, cq_hi = split2(cq, p)
            ck_lo, ck_hi = split2(ck, p)
            q_lo, q_hi = split2(q, p)
            k_lo, k_hi = split2(k, p)
            qu = (q_hi * cq_hi).reshape(w // 2, HG_DK).astype(BF16)
            kn = merge2(k_lo * ck_lo, k_hi).astype(BF16)
            new_a = []
            for c in range(n_half):
                s_c = _dot_nt(qu[c * (hw // 2):(c + 1) * (hw // 2)], kn[c * hw:(c + 1) * hw])
                a_lo, a_hi = split2(a_diag[c], p)
                _, lv_hi = split2(lvl_ref[...], p)
                a_hi = jnp.where(lv_hi == level, s_c.reshape(a_hi.shape), a_hi)
                new_a.append(merge2(a_lo, a_hi))
            a_diag = new_a
            nb = w // (2 * p)
            r4 = r.reshape(nb, 2, sub, HG_DK)
            r_lo, r_hi = r4[:, 0], r4[:, 1]
            cq_hi = (cq_hi.reshape(nb, p // sub, sub, HG_DK) * r_lo[:, None]).reshape(nb, p, HG_DK)
            ck_lo = (ck_lo.reshape(nb, p // sub, sub, HG_DK) * r_hi[:, None]).reshape(nb, p, HG_DK)
            cq = merge2(cq_lo, cq_hi)
            ck = merge2(ck_lo, ck_hi)
            r = r_lo * r_hi
            p *= 2
            level += 1

        r_lo, r_hi = r[0], r[1]
        qu = (q[hw:] * cq[hw:]).astype(BF16)
        kl = (k[:hw] * ck[:hw]).astype(BF16)
        a_cross = _dot_nt(qu, kl)
        cq = jnp.concatenate(
            [cq[:hw], (cq[hw:].reshape(hw // sub, sub, HG_DK) * r_lo[None]).reshape(hw, HG_DK)], axis=0)
        ck = jnp.concatenate(
            [(ck[:hw].reshape(hw // sub, sub, HG_DK) * r_hi[None]).reshape(hw, HG_DK), ck[hw:]], axis=0)
        r_tot = r_lo * r_hi

        st = st_ref[h]
        vb = v.astype(BF16)
        o_inter = _dot_nt((q * cq).astype(BF16), st.astype(BF16))
        o0 = _dot(a_diag[0].astype(BF16), vb[:hw])
        o1 = _dot(jnp.concatenate([a_cross, a_diag[1]], axis=1).astype(BF16), vb)
        o = jnp.concatenate([o0, o1], axis=0) + o_inter
        st_ref[h] = st * r_tot[0:1, :] + _dot(v.T.astype(BF16), (k * ck).astype(BF16))

        on = _rms(o) * nw_ref[:, vs]
        o_ref[:, vs] = (on * _silu(gate)).astype(o_ref.dtype)
        return carry

    lax.fori_loop(0, HG_HEADS, head, 0, unroll=HGRN_HEADS_PER_ITER)


def _hgrn2(proj, lb_logits, norm_w, batch, seqlen, layer):
    w = HGRN_WINDOW
    nw = seqlen // w
    blocks = 4 * _nbytes((w, HG_KEY), proj.dtype) + _nbytes((w, HG_VAL), MIX_DTYPE)
    scratch = (_nbytes((HG_HEADS, HG_DV, HG_DK), F32) + _nbytes((HGRN_HALF, HGRN_HALF), jnp.int32)
               + _nbytes((V7X_SUBLANES, HG_KEY), F32))

    def col(c):
        return pl.BlockSpec((w, HG_KEY), lambda b, i: (b * nw + i, c))

    return pl.pallas_call(
        functools.partial(_hgrn_body, layer=layer),
        grid=(batch, nw),
        in_specs=[
            col(0), col(1), col(2), col(3),
            pl.BlockSpec(lb_logits.shape, lambda b, i: (0, 0)),
            pl.BlockSpec((1, HG_VAL), lambda b, i: (0, 0)),
        ],
        out_specs=pl.BlockSpec((w, HG_VAL), lambda b, i: (b * nw + i, 0)),
        out_shape=jax.ShapeDtypeStruct((batch * seqlen, HG_VAL), MIX_DTYPE),
        scratch_shapes=[
            pltpu.VMEM((HG_HEADS, HG_DV, HG_DK), F32),
            pltpu.VMEM((HGRN_HALF, HGRN_HALF), jnp.int32),
            pltpu.VMEM((1, HG_KEY), F32),
        ],
        compiler_params=pltpu.CompilerParams(
            dimension_semantics=("parallel", "arbitrary"),
            vmem_limit_bytes=_vmem_limit(blocks, scratch),
        ),
        name="hgrn2",
    )(proj, proj, proj, proj, lb_logits, norm_w)


def _split3(x):
    hi = x.astype(BF16)
    r1 = x - hi.astype(F32)
    mid = r1.astype(BF16)
    lo = (r1 - mid.astype(F32)).astype(BF16)
    return hi, mid, lo


def _dot_expand(x, m01):
    hi = x.astype(BF16)
    mid = (x - hi.astype(F32)).astype(BF16)
    return _dot(hi, m01) + _dot(mid, m01)


def _ssd_body(z_ref, x0_ref, x1_ref, bc_ref, dt_ref, cw_ref, cb_ref, dtb_ref, alog_ref,
              dskip_ref, nw_ref, expand_ref, o_ref, xbuf_ref, state_ref):
    wc = SSD_CHUNK
    hdr = V7X_SUBLANES
    gw = SSD_GROUP_WIDTH

    @pl.when(pl.program_id(1) == 0)
    def _():
        state_ref[...] = jnp.zeros_like(state_ref)
        xbuf_ref[:, 0:hdr, :] = jnp.zeros((xbuf_ref.shape[0], hdr, V7X_LANES), F32)

    slabs = gw // V7X_LANES
    conv = []
    for pi, ref in enumerate((x0_ref, x1_ref, bc_ref)):
        cols = []
        for sl in range(slabs):
            si = pi * slabs + sl
            cs = slice(pi * gw + sl * V7X_LANES, pi * gw + (sl + 1) * V7X_LANES)
            xbuf_ref[si, hdr:hdr + wc, :] = ref[:, sl * V7X_LANES:(sl + 1) * V7X_LANES].astype(F32)
            acc = cb_ref[:, cs]
            for kk in range(SSD_CONV):
                off = hdr - (SSD_CONV - 1) + kk
                acc = acc + cw_ref[kk:kk + 1, cs] * xbuf_ref[si, off:off + wc, :]
            xbuf_ref[si, 0:hdr, :] = xbuf_ref[si, wc:wc + hdr, :]
            cols.append(_silu(acc))
        conv.append(jnp.concatenate(cols, axis=1))
    xs = conv[0:2]
    bcs = conv[2]

    dtr = dt_ref[...] + dtb_ref[...]
    dt = jnp.maximum(dtr, 0.0) + jnp.log1p(jnp.exp(-jnp.abs(dtr)))
    a = dt * (-jnp.exp(alog_ref[...]))
    t_idx = lax.broadcasted_iota(jnp.int32, (wc, wc), 0)
    s_idx = lax.broadcasted_iota(jnp.int32, (wc, wc), 1)
    causal = t_idx >= s_idx
    tril01 = jnp.where(causal, 1.0, 0.0).astype(BF16)
    acs = _dot_exact01_left(tril01, a) * LOG2_E
    acs_t = acs.T

    expand = expand_ref[...]
    acs_x = _dot_expand(acs, expand)
    dt_x = _dot_expand(dt, expand)
    lane = lax.broadcasted_iota(jnp.int32, (wc, V7X_LANES), 1)
    first_half = lane < SSD_HEADDIM

    for g in range(SSD_GROUPS):
        gs = slice(g * gw, (g + 1) * gw)
        xs_g = xs[g]
        xdt = xs_g * dt_x[:, gs]
        xdt_b = xdt.astype(BF16)
        b_g = bcs[:, g * SSD_STATE:(g + 1) * SSD_STATE]
        c_g = bcs[:, (SSD_GROUPS + g) * SSD_STATE:(SSD_GROUPS + g + 1) * SSD_STATE]
        c_b = c_g.astype(BF16)
        cb = _dot_nt(c_b, b_g.astype(BF16))

        pieces = []
        for j in range(SSD_HPG // 2):
            xp = xdt_b[:, j * V7X_LANES:(j + 1) * V7X_LANES]
            acc = None
            for half in range(2):
                h = g * SSD_HPG + 2 * j + half
                seg = acs[:, h:h + 1] - acs_t[h:h + 1, :]
                m = (cb * jnp.exp2(jnp.where(causal, seg, NEG_BIG))).astype(BF16)
                keep = first_half if half == 0 else jnp.logical_not(first_half)
                part = _dot(m, jnp.where(keep, xp, jnp.zeros_like(xp)))
                acc = part if acc is None else acc + part
            pieces.append(acc)
        y_diag = jnp.concatenate(pieces, axis=-1)

        st = state_ref[g]
        acs_g = acs_x[:, gs]
        y_off = _dot(c_b, st.astype(BF16)) * jnp.exp2(acs_g)
        y = y_diag + y_off + dskip_ref[:, gs] * xs_g

        last = acs_g[wc - 1:wc, :]
        xdec = (xdt * jnp.exp2(last - acs_g)).astype(BF16)
        state_ref[g] = st * jnp.exp2(last) + _dot(b_g.T.astype(BF16), xdec)

        yz = y * _silu(z_ref[:, gs].astype(F32))
        o_ref[:, gs] = (_rms(yz) * nw_ref[:, gs]).astype(o_ref.dtype)


def _dot_exact01_left(m01, x):
    hi, mid, lo = _split3(x)
    return _dot(m01, hi) + _dot(m01, mid) + _dot(m01, lo)


def _ssd(proj, dt_raw, conv_w, conv_b, dt_bias_p, a_log_p, d_skip_x, norm_w, expand, batch, seqlen):
    wc = SSD_CHUNK
    nc = seqlen // wc
    gw = SSD_GROUP_WIDTH
    z_blk = (HG_KEY * 2 + HG_VAL * 2) // SSD_WIDTH
    x_blk = (HG_KEY * 2 + HG_VAL * 2 + SSD_WIDTH) // gw
    hdr = V7X_SUBLANES
    blocks = (_nbytes((wc, SSD_WIDTH), proj.dtype) + 3 * _nbytes((wc, gw), proj.dtype)
              + _nbytes((wc, V7X_LANES), F32) + _nbytes((wc, SSD_WIDTH), MIX_DTYPE)
              + _nbytes((V7X_LANES, SSD_WIDTH), BF16))
    xbuf_shape = (3 * gw // V7X_LANES, hdr + wc, V7X_LANES)
    scratch = _nbytes(xbuf_shape, F32) + _nbytes((SSD_GROUPS, SSD_STATE, gw), F32)
    values = 8 * _nbytes((wc, SSD_WIDTH), F32)

    def tok(width, c):
        return pl.BlockSpec((wc, width), lambda b, i: (b * nc + i, c))

    def whole(arr):
        return pl.BlockSpec(arr.shape, lambda b, i: (0,) * arr.ndim)

    return pl.pallas_call(
        _ssd_body,
        grid=(batch, nc),
        in_specs=[
            tok(SSD_WIDTH, z_blk), tok(gw, x_blk), tok(gw, x_blk + 1), tok(gw, x_blk + 2),
            tok(V7X_LANES, 0),
            whole(conv_w), whole(conv_b), whole(dt_bias_p), whole(a_log_p), whole(d_skip_x),
            whole(norm_w), whole(expand),
        ],
        out_specs=pl.BlockSpec((wc, SSD_WIDTH), lambda b, i: (b * nc + i, 0)),
        out_shape=jax.ShapeDtypeStruct((batch * seqlen, SSD_WIDTH), MIX_DTYPE),
        scratch_shapes=[
            pltpu.VMEM(xbuf_shape, F32),
            pltpu.VMEM((SSD_GROUPS, SSD_STATE, gw), F32),
        ],
        compiler_params=pltpu.CompilerParams(
            dimension_semantics=("parallel", "arbitrary"),
            vmem_limit_bytes=_vmem_limit(blocks, scratch, values),
        ),
        name="ssd",
    )(proj, proj, proj, proj, dt_raw, conv_w, conv_b, dt_bias_p, a_log_p, d_skip_x, norm_w, expand)


def _outproj_body(oa_ref, ob_ref, wa_ref, wb_ref, x_ref, postw_ref, prew_ref, x1_ref, h2_ref):
    mix = _dot(oa_ref[...], wa_ref[...]) + _dot(ob_ref[...], wb_ref[...])
    x1 = x_ref[...] + _rms(mix) * postw_ref[...]
    x1_ref[...] = x1
    h2_ref[...] = (_rms(x1) * prew_ref[...]).astype(h2_ref.dtype)


def _out_proj(o_a, o_b, w_a, w_b, x2, post_w, pre_w):
    t, d = x2.shape
    tm = OUTPROJ_TM
    blocks = (_nbytes((tm, HG_VAL), MIX_DTYPE) + _nbytes((tm, SSD_WIDTH), MIX_DTYPE)
              + _nbytes(w_a.shape, BF16) + _nbytes(w_b.shape, BF16) + 2 * _nbytes((tm, d), F32)
              + _nbytes((tm, d), BF16))

    def row(i):
        return (i, 0)

    def fixed(i):
        return (0, 0)

    return pl.pallas_call(
        _outproj_body,
        grid=(t // tm,),
        in_specs=[
            pl.BlockSpec((tm, HG_VAL), row),
            pl.BlockSpec((tm, SSD_WIDTH), row),
            pl.BlockSpec(w_a.shape, fixed),
            pl.BlockSpec(w_b.shape, fixed),
            pl.BlockSpec((tm, d), row),
            pl.BlockSpec((1, d), fixed),
            pl.BlockSpec((1, d), fixed),
        ],
        out_specs=[pl.BlockSpec((tm, d), row), pl.BlockSpec((tm, d), row)],
        out_shape=[jax.ShapeDtypeStruct((t, d), F32), jax.ShapeDtypeStruct((t, d), BF16)],
        compiler_params=pltpu.CompilerParams(
            dimension_semantics=("parallel",),
            vmem_limit_bytes=_vmem_limit(blocks),
        ),
        name="out_proj",
    )(o_a, o_b, w_a, w_b, x2, post_w, pre_w)


def _ffn_body(h_ref, x1c_ref, wg_ref, wu_ref, wd_ref, postw_ref, o_ref, x1_ref):
    j = pl.program_id(1)
    d = o_ref.shape[1]
    xc = x1c_ref.shape[1]

    @pl.when(j == 0)
    def _():
        o_ref[...] = jnp.zeros_like(o_ref)

    @pl.when(j < d // xc)
    def _():
        x1_ref[:, pl.ds(pl.multiple_of(j * xc, xc), xc)] = x1c_ref[...]

    h = h_ref[...]
    hids = []
    for s in range(FFN_TF // FFN_SUB):
        cs = slice(s * FFN_SUB, (s + 1) * FFN_SUB)
        g = _dot(h, wg_ref[:, cs])
        u = _dot(h, wu_ref[:, cs])
        hids.append((_silu(g) * u).astype(BF16))
    hid = jnp.concatenate(hids, axis=1)
    for c in range(d // FFN_DOWN_COLS):
        cs = slice(c * FFN_DOWN_COLS, (c + 1) * FFN_DOWN_COLS)
        o_ref[:, cs] += _dot(hid, wd_ref[:, cs])

    @pl.when(j == pl.num_programs(1) - 1)
    def _():
        o_ref[...] = x1_ref[...] + _rms(o_ref[...]) * postw_ref[...]


def _ffn(h2, x1, w_gate, w_up, w_down, post_w):
    t, d = x1.shape
    f = w_gate.shape[1]
    tm, tf, xc = FFN_TM, FFN_TF, FFN_X1_COLS
    n_xc = d // xc
    assert f // tf >= n_xc
    blocks = (_nbytes((tm, d), BF16) + _nbytes((tm, xc), F32) + 2 * _nbytes((d, tf), BF16)
              + _nbytes((tf, d), BF16) + _nbytes((tm, d), F32))
    values = (2 * _nbytes((tm, tf), F32) + _nbytes((tm, tf), BF16) + _nbytes((tm, d), F32))
    return pl.pallas_call(
        _ffn_body,
        grid=(t // tm, f // tf),
        in_specs=[
            pl.BlockSpec((tm, d), lambda i, j: (i, 0)),
            pl.BlockSpec((tm, xc), lambda i, j: (i, jnp.minimum(j, n_xc - 1))),
            pl.BlockSpec((d, tf), lambda i, j: (0, j)),
            pl.BlockSpec((d, tf), lambda i, j: (0, j)),
            pl.BlockSpec((tf, d), lambda i, j: (j, 0)),
            pl.BlockSpec((1, d), lambda i, j: (0, 0)),
        ],
        out_specs=pl.BlockSpec((tm, d), lambda i, j: (i, 0)),
        out_shape=jax.ShapeDtypeStruct((t, d), F32),
        scratch_shapes=[pltpu.VMEM((tm, d), F32)],
        compiler_params=pltpu.CompilerParams(
            dimension_semantics=("parallel", "arbitrary"),
            vmem_limit_bytes=_vmem_limit(blocks, _nbytes((tm, d), F32), values),
        ),
        name="ffn",
    )(h2, x1, w_gate, w_up, w_down, post_w)


def _pad_lanes(v):
    return jnp.pad(v.astype(F32), (0, V7X_LANES - v.shape[0]))[None, :]


def kernel(x, pre_mix_norm_w, w_in, lb_logits, conv_w, conv_b, dt_bias, a_log, d_skip, hgrn_norm_w,
           ssd_norm_w, w_out, post_mix_norm_w, pre_ffn_norm_w, w_gate, w_up, w_down, post_ffn_norm_w):
    batch, seqlen, d = x.shape
    depth = w_in.shape[0]
    n_main = 2 * HG_KEY + 2 * HG_VAL + SSD_WIDTH + SSD_WIDTH + SSD_BC_WIDTH
    expand = (jnp.arange(V7X_LANES)[:, None] == (jnp.arange(SSD_WIDTH)[None, :] // SSD_HEADDIM)).astype(BF16)

    x2 = x.reshape(batch * seqlen, d)
    for l in range(depth):
        w_main = w_in[l, :, :n_main].astype(BF16)
        w_dt = jnp.pad(w_in[l, :, n_main:], ((0, 0), (0, V7X_LANES - SSD_HEADS))).astype(BF16)
        proj, dt_raw = _in_proj(x2, pre_mix_norm_w[l][None, :], w_main, w_dt)
        o_a = _hgrn2(proj, lb_logits.astype(F32), hgrn_norm_w[l][None, :], batch, seqlen, l)
        o_b = _ssd(proj, dt_raw, conv_w[l], conv_b[l][None, :], _pad_lanes(dt_bias[l]),
                   _pad_lanes(a_log[l]), jnp.repeat(d_skip[l].astype(F32), SSD_HEADDIM)[None, :],
                   ssd_norm_w[l][None, :], expand, batch, seqlen)
        x1, h2 = _out_proj(o_a, o_b, w_out[l, :HG_VAL].astype(BF16), w_out[l, HG_VAL:].astype(BF16),
                           x2, post_mix_norm_w[l][None, :], pre_ffn_norm_w[l][None, :])
        x2 = _ffn(h2, x1, w_gate[l].astype(BF16), w_up[l].astype(BF16), w_down[l].astype(BF16),
                  post_ffn_norm_w[l][None, :])
    return x2.reshape(batch, seqlen, d)
```

```python
import functools

import jax
import jax.numpy as jnp
from jax import lax
from jax.experimental import pallas as pl
from jax.experimental.pallas import tpu as pltpu

F32 = jnp.float32
BF16 = jnp.bfloat16

HG_HEADS = 8
HG_DK = 128
HG_DV = 128
HG_KEY = HG_HEADS * HG_DK
HG_VAL = HG_HEADS * HG_DV
SSD_HEADS = 16
SSD_HEADDIM = 64
SSD_WIDTH = SSD_HEADS * SSD_HEADDIM
SSD_GROUPS = 2
SSD_HPG = SSD_HEADS // SSD_GROUPS
SSD_STATE = 128
SSD_CONV = 4
SSD_GROUP_WIDTH = SSD_HPG * SSD_HEADDIM
SSD_BC_WIDTH = 2 * SSD_GROUPS * SSD_STATE
NORM_EPS = 1e-6

V7X_LANES = 128
V7X_SUBLANES = 8
V7X_VMEM_BYTES = 64 * 1024 * 1024
V7X_VMEM_COMPILER_RESERVE = 6 * 1024 * 1024
V7X_VMEM_UNSCOPED = 2 * 1024 * 1024

MIX_DTYPE = BF16
MIX_CHUNK = 256
HGRN_HALF = 128
HGRN_HEADS_PER_GROUP = 4
MIX_PIECE_COLS = 256
HG_HEAD_COLS = 2 * HG_DK + 2 * HG_DV
SSD_COLS = SSD_WIDTH + SSD_WIDTH + SSD_BC_WIDTH + V7X_LANES
OUTPROJ_TM = 512
FFN_TM = 1024
FFN_TF = 512
FFN_SUB = 256
FFN_DOWN_COLS = 512
FFN_X1_COLS = 256
NEG_BIG = -1e30
LOG2_E = 1.4426950408889634


def _nbytes(shape, dtype):
    n = 1
    for s in shape:
        n *= s
    return n * jnp.dtype(dtype).itemsize


def _vmem_limit(block_bytes, scratch_bytes=0, value_bytes=0):
    need = 2 * block_bytes + scratch_bytes + value_bytes + V7X_VMEM_COMPILER_RESERVE
    return int(min(need, V7X_VMEM_BYTES - V7X_VMEM_UNSCOPED))


def _sigmoid(x):
    return 1.0 / (1.0 + jnp.exp2(x * (-LOG2_E)))


def _silu(x):
    return x * _sigmoid(x)


def _rms(x):
    return x * lax.rsqrt(jnp.mean(x * x, axis=-1, keepdims=True) + NORM_EPS)


def _dot(a, b):
    return jnp.dot(a, b, preferred_element_type=F32)


def _dot_nt(a, b):
    return lax.dot_general(a, b, (((1,), (1,)), ((), ())), preferred_element_type=F32)


def _split2(x, p):
    x4 = x.reshape(x.shape[0] // (2 * p), 2, p, x.shape[1])
    return x4[:, 0], x4[:, 1]


def _merge2(lo, hi):
    return jnp.stack([lo, hi], axis=1).reshape(-1, lo.shape[-1])


def _hgrn_group(heads, lvl_ref, filler):
    g = len(heads)
    w = MIX_CHUNK
    hw = HGRN_HALF
    sub = V7X_SUBLANES
    rows = g * w
    n_half = rows // hw
    assert w == 2 * hw and hw == V7X_LANES
    shp3 = (rows // sub, sub, HG_DK)
    sub_idx = lax.broadcasted_iota(jnp.int32, (1, sub, HG_DK), 1)

    q = jnp.concatenate([_silu(h[0]) * (HG_DK ** -0.5) for h in heads], axis=0)
    f = jnp.concatenate([h[4] + (1.0 - h[4]) * _sigmoid(h[1]) for h in heads], axis=0)
    k = 1.0 - f

    def half_scores(qn, kn, level, a_blocks):
        qb, kb = qn.astype(BF16), kn.astype(BF16)
        out = []
        for c in range(n_half):
            rs = slice(c * hw, (c + 1) * hw)
            s_c = _dot_nt(qb[rs], kb[rs])
            keep = lvl_ref[...] == level
            out.append(jnp.where(keep, s_c, 0.0 if a_blocks is None else a_blocks[c]))
        return out

    a_diag = half_scores(q, k, 0, None)
    filler()
    cq = f.reshape(shp3)
    ck = jnp.ones(shp3, F32)
    r = cq
    q3, k3 = q.reshape(shp3), k.reshape(shp3)
    level = 1
    p = 1
    while p < sub:
        a_diag = half_scores((q3 * cq).reshape(rows, HG_DK), (k3 * ck).reshape(rows, HG_DK), level, a_diag)
        upper = (sub_idx & p) != 0
        down = pltpu.roll(r, p, axis=1)
        up = pltpu.roll(r, sub - p, axis=1)
        cq = cq * jnp.where(upper, down, 1.0)
        ck = ck * jnp.where(upper, 1.0, up)
        r = r * jnp.where(upper, down, up)
        filler()
        p *= 2
        level += 1
    cq = cq.reshape(rows, HG_DK)
    ck = ck.reshape(rows, HG_DK)

    a_cross = None
    while p < w:
        cq_lo, cq_hi = _split2(cq, p)
        ck_lo, ck_hi = _split2(ck, p)
        q_lo, q_hi = _split2(q, p)
        k_lo, k_hi = _split2(k, p)
        nb = rows // (2 * p)
        if p < hw:
            qu = (q_hi * cq_hi).reshape(rows // 2, HG_DK).astype(BF16)
            kn = _merge2(k_lo * ck_lo, k_hi).astype(BF16)
            new_a = []
            for c in range(n_half):
                s_c = _dot_nt(qu[c * (hw // 2):(c + 1) * (hw // 2)], kn[c * hw:(c + 1) * hw])
                a_lo, a_hi = _split2(a_diag[c], p)
                _, lv_hi = _split2(lvl_ref[...], p)
                a_hi = jnp.where(lv_hi == level, s_c.reshape(a_hi.shape), a_hi)
                new_a.append(_merge2(a_lo, a_hi))
            a_diag = new_a
        else:
            qu = (q_hi * cq_hi).astype(BF16)
            kl = (k_lo * ck_lo).astype(BF16)
            a_cross = [_dot_nt(qu[i], kl[i]) for i in range(g)]
        r4 = r.reshape(nb, 2, sub, HG_DK)
        r_lo, r_hi = r4[:, 0], r4[:, 1]
        cq_hi = (cq_hi.reshape(nb, p // sub, sub, HG_DK) * r_lo[:, None]).reshape(nb, p, HG_DK)
        ck_lo = (ck_lo.reshape(nb, p // sub, sub, HG_DK) * r_hi[:, None]).reshape(nb, p, HG_DK)
        cq = _merge2(cq_lo, cq_hi)
        ck = _merge2(ck_lo, ck_hi)
        r = r_lo * r_hi
        filler()
        p *= 2
        level += 1

    qn = (q * cq).astype(BF16)
    kn = (k * ck).astype(BF16)
    results = []
    for i, (_, _, v, gate, _, nw, st) in enumerate(heads):
        rs = slice(i * w, (i + 1) * w)
        vb = v.astype(BF16)
        o_inter = _dot_nt(qn[rs], st.astype(BF16))
        o0 = _dot(a_diag[2 * i].astype(BF16), vb[:hw])
        o1 = _dot(jnp.concatenate([a_cross[i], a_diag[2 * i + 1]], axis=1).astype(BF16), vb)
        o = jnp.concatenate([o0, o1], axis=0) + o_inter
        st_new = st * r[i, 0:1, :] + _dot(v.T.astype(BF16), kn[rs])
        results.append(((_rms(o) * nw * _silu(gate)).astype(MIX_DTYPE), st_new))
    return results


def _split3(x):
    hi = x.astype(BF16)
    r1 = x - hi.astype(F32)
    mid = r1.astype(BF16)
    lo = (r1 - mid.astype(F32)).astype(BF16)
    return hi, mid, lo


def _dot_exact01_left(m01, x):
    hi, mid, lo = _split3(x)
    return _dot(m01, hi) + _dot(m01, mid) + _dot(m01, lo)


def _dot_expand(x, m01):
    hi = x.astype(BF16)
    mid = (x - hi.astype(F32)).astype(BF16)
    return _dot(hi, m01) + _dot(mid, m01)


def _ssd_chunk(ps, cw_ref, cb_ref, dtb_ref, alog_ref, dskip_ref, nw_ref, expand_ref, o_ref,
               xbuf_ref, state_ref, filler):
    wc = MIX_CHUNK
    hdr = V7X_SUBLANES
    gw = SSD_GROUP_WIDTH
    z = ps[:, :SSD_WIDTH]
    xbc = ps[:, SSD_WIDTH:SSD_WIDTH + SSD_WIDTH + SSD_BC_WIDTH]
    dt_raw = ps[:, SSD_WIDTH + SSD_WIDTH + SSD_BC_WIDTH:]

    n_slabs = xbc.shape[1] // V7X_LANES
    cols = []
    for si in range(n_slabs):
        cs = slice(si * V7X_LANES, (si + 1) * V7X_LANES)
        xbuf_ref[si, hdr:hdr + wc, :] = xbc[:, cs]
        acc = cb_ref[:, cs]
        for kk in range(SSD_CONV):
            off = hdr - (SSD_CONV - 1) + kk
            acc = acc + cw_ref[kk:kk + 1, cs] * xbuf_ref[si, off:off + wc, :]
        xbuf_ref[si, 0:hdr, :] = xbuf_ref[si, wc:wc + hdr, :]
        cols.append(_silu(acc))
    per_group = gw // V7X_LANES
    xs = [jnp.concatenate(cols[g * per_group:(g + 1) * per_group], axis=1) for g in range(SSD_GROUPS)]
    bcs = jnp.concatenate(cols[SSD_GROUPS * per_group:], axis=1)

    dtr = dt_raw + dtb_ref[...]
    dt = jnp.maximum(dtr, 0.0) + jnp.log1p(jnp.exp(-jnp.abs(dtr)))
    a = dt * (-jnp.exp(alog_ref[...]))
    t_idx = lax.broadcasted_iota(jnp.int32, (wc, wc), 0)
    s_idx = lax.broadcasted_iota(jnp.int32, (wc, wc), 1)
    causal = t_idx >= s_idx
    tril01 = jnp.where(causal, 1.0, 0.0).astype(BF16)
    acs = _dot_exact01_left(tril01, a) * LOG2_E
    acs_t = acs.T

    expand = expand_ref[...]
    acs_x = _dot_expand(acs, expand)
    dt_x = _dot_expand(dt, expand)
    lane = lax.broadcasted_iota(jnp.int32, (wc, V7X_LANES), 1)
    first_half = lane < SSD_HEADDIM

    for g in range(SSD_GROUPS):
        gs = slice(g * gw, (g + 1) * gw)
        xs_g = xs[g]
        xdt = xs_g * dt_x[:, gs]
        xdt_b = xdt.astype(BF16)
        b_g = bcs[:, g * SSD_STATE:(g + 1) * SSD_STATE]
        c_g = bcs[:, (SSD_GROUPS + g) * SSD_STATE:(SSD_GROUPS + g + 1) * SSD_STATE]
        c_b = c_g.astype(BF16)
        cb = _dot_nt(c_b, b_g.astype(BF16))

        pieces = []
        for j in range(SSD_HPG // 2):
            filler()
            xp = xdt_b[:, j * V7X_LANES:(j + 1) * V7X_LANES]
            acc = None
            for half in range(2):
                h = g * SSD_HPG + 2 * j + half
                seg = acs[:, h:h + 1] - acs_t[h:h + 1, :]
                m = (cb * jnp.exp2(jnp.where(causal, seg, NEG_BIG))).astype(BF16)
                keep = first_half if half == 0 else jnp.logical_not(first_half)
                part = _dot(m, jnp.where(keep, xp, jnp.zeros_like(xp)))
                acc = part if acc is None else acc + part
            pieces.append(acc)
        y_diag = jnp.concatenate(pieces, axis=-1)

        st = state_ref[g]
        acs_g = acs_x[:, gs]
        y_off = _dot(c_b, st.astype(BF16)) * jnp.exp2(acs_g)
        y = y_diag + y_off + dskip_ref[:, gs] * xs_g

        last = acs_g[wc - 1:wc, :]
        xdec = (xdt * jnp.exp2(last - acs_g)).astype(BF16)
        state_ref[g] = st * jnp.exp2(last) + _dot(b_g.T.astype(BF16), xdec)

        yz = y * _silu(z[:, gs])
        o_ref[:, gs] = (_rms(yz) * nw_ref[:, gs]).astype(o_ref.dtype)


def _mixer_body(x_ref, prew_ref, w_ref, lbl_ref, hnw_ref, cw_ref, cb_ref, dtb_ref, alog_ref,
                dskip_ref, snw_ref, expand_ref, oa_ref, ob_ref,
                hst_ref, lvl_ref, lb_ref, xbuf_ref, sst_ref, pj_ref, *, layer, chunks_per_seq):
    hw = HGRN_HALF
    s = pl.program_id(0)
    grp = HGRN_HEADS_PER_GROUP
    ssd_lo = HG_HEADS * HG_HEAD_COLS
    n_all = w_ref.shape[1]

    @pl.when(s == 0)
    def _():
        pj_ref[...] = jnp.zeros_like(pj_ref)
        t_idx = lax.broadcasted_iota(jnp.int32, (hw, hw), 0)
        s_idx = lax.broadcasted_iota(jnp.int32, (hw, hw), 1)
        lv = 32 - lax.clz(t_idx ^ s_idx)
        lvl_ref[...] = jnp.where(t_idx >= s_idx, lv, -1)
        lg = lbl_ref[...]
        e = jnp.exp(lg - jnp.max(lg, axis=0, keepdims=True))
        sm = e / jnp.sum(e, axis=0, keepdims=True)
        lb_ref[...] = jnp.sum(sm[: layer + 1], axis=0, keepdims=True)

    @pl.when(jnp.logical_or(s == 0, lax.rem(s - 1, chunks_per_seq) == 0))
    def _():
        hst_ref[...] = jnp.zeros_like(hst_ref)
        sst_ref[...] = jnp.zeros_like(sst_ref)
        xbuf_ref[:, 0:V7X_SUBLANES, :] = jnp.zeros((xbuf_ref.shape[0], V7X_SUBLANES, V7X_LANES), F32)

    hb = (_rms(x_ref[...]) * prew_ref[...]).astype(BF16)
    todo = [(lo, min(MIX_PIECE_COLS, n_all - lo)) for lo in range(0, n_all, MIX_PIECE_COLS)]
    new_proj = []

    def filler():
        if todo:
            lo, width = todo.pop(0)
            new_proj.append((lo, width, _dot(hb, w_ref[:, lo:lo + width])))

    _ssd_chunk(pj_ref[:, ssd_lo:], cw_ref, cb_ref, dtb_ref, alog_ref, dskip_ref, snw_ref, expand_ref,
               ob_ref, xbuf_ref, sst_ref, filler)
    for gi in range(HG_HEADS // grp):
        heads = []
        for g in range(grp):
            h = gi * grp + g
            base = h * HG_HEAD_COLS
            ks = slice(h * HG_DK, (h + 1) * HG_DK)
            vs = slice(h * HG_DV, (h + 1) * HG_DV)
            heads.append((pj_ref[:, base:base + HG_DK], pj_ref[:, base + HG_DK:base + 2 * HG_DK],
                          pj_ref[:, base + 2 * HG_DK:base + 2 * HG_DK + HG_DV],
                          pj_ref[:, base + 2 * HG_DK + HG_DV:base + HG_HEAD_COLS],
                          lb_ref[:, ks], hnw_ref[:, vs], hst_ref[h]))
        for g, (out, st_new) in enumerate(_hgrn_group(heads, lvl_ref, filler)):
            h = gi * grp + g
            hst_ref[h] = st_new
            oa_ref[:, h * HG_DV:(h + 1) * HG_DV] = out
    while todo:
        filler()

    for lo, width, val in new_proj:
        pj_ref[:, lo:lo + width] = val


def _mixer(x2, pre_w, w_all, lb_logits, hgrn_nw, conv_w, conv_b, dt_bias_p, a_log_p, d_skip_x,
           ssd_nw, expand, seqlen, layer):
    t, d = x2.shape
    wc = MIX_CHUNK
    n_chunks = t // wc
    n_all = w_all.shape[1]
    assert n_all == HG_HEADS * HG_HEAD_COLS + SSD_COLS
    xbuf_shape = ((SSD_WIDTH + SSD_BC_WIDTH) // V7X_LANES, V7X_SUBLANES + wc, V7X_LANES)
    blocks = _nbytes((wc, d), F32) + _nbytes((wc, HG_VAL), MIX_DTYPE) + _nbytes((wc, SSD_WIDTH), MIX_DTYPE)
    resident = _nbytes(w_all.shape, BF16) + 2 * _nbytes(expand.shape, BF16)
    scratch = (_nbytes((HG_HEADS, HG_DV, HG_DK), F32) + _nbytes((HGRN_HALF, HGRN_HALF), jnp.int32)
               + _nbytes((V7X_SUBLANES, HG_KEY), F32) + _nbytes(xbuf_shape, F32)
               + _nbytes((SSD_GROUPS, SSD_STATE, SSD_GROUP_WIDTH), F32) + _nbytes((wc, n_all), F32))
    values = _nbytes((wc, d), BF16) + _nbytes((wc, n_all), F32) + 8 * _nbytes((wc, SSD_WIDTH), F32)

    def whole(arr, **kw):
        return pl.BlockSpec(arr.shape, lambda s: (0,) * arr.ndim, **kw)

    def finished(width):
        return pl.BlockSpec((wc, width), lambda s: (jnp.maximum(s - 1, 0), 0))

    return pl.pallas_call(
        functools.partial(_mixer_body, layer=layer, chunks_per_seq=seqlen // wc),
        grid=(n_chunks + 1,),
        in_specs=[
            pl.BlockSpec((wc, d), lambda s: (jnp.minimum(s, n_chunks - 1), 0)),
            whole(pre_w),
            whole(w_all, pipeline_mode=pl.Buffered(1)),
            whole(lb_logits), whole(hgrn_nw), whole(conv_w), whole(conv_b), whole(dt_bias_p),
            whole(a_log_p), whole(d_skip_x), whole(ssd_nw), whole(expand),
        ],
        out_specs=[finished(HG_VAL), finished(SSD_WIDTH)],
        out_shape=[jax.ShapeDtypeStruct((t, HG_VAL), MIX_DTYPE),
                   jax.ShapeDtypeStruct((t, SSD_WIDTH), MIX_DTYPE)],
        scratch_shapes=[
            pltpu.VMEM((HG_HEADS, HG_DV, HG_DK), F32),
            pltpu.VMEM((HGRN_HALF, HGRN_HALF), jnp.int32),
            pltpu.VMEM((1, HG_KEY), F32),
            pltpu.VMEM(xbuf_shape, F32),
            pltpu.VMEM((SSD_GROUPS, SSD_STATE, SSD_GROUP_WIDTH), F32),
            pltpu.VMEM((wc, n_all), F32),
        ],
        compiler_params=pltpu.CompilerParams(
            dimension_semantics=("arbitrary",),
            vmem_limit_bytes=_vmem_limit(blocks, resident + scratch, values),
        ),
        name="mixer",
    )(x2, pre_w, w_all, lb_logits, hgrn_nw, conv_w, conv_b, dt_bias_p, a_log_p, d_skip_x, ssd_nw, expand)


def _outproj_body(oa_ref, ob_ref, wa_ref, wb_ref, x_ref, postw_ref, prew_ref, x1_ref, h2_ref):
    mix = _dot(oa_ref[...], wa_ref[...]) + _dot(ob_ref[...], wb_ref[...])
    x1 = x_ref[...] + _rms(mix) * postw_ref[...]
    x1_ref[...] = x1
    h2_ref[...] = (_rms(x1) * prew_ref[...]).astype(h2_ref.dtype)


def _out_proj(o_a, o_b, w_a, w_b, x2, post_w, pre_w):
    t, d = x2.shape
    tm = OUTPROJ_TM
    blocks = (_nbytes((tm, HG_VAL), MIX_DTYPE) + _nbytes((tm, SSD_WIDTH), MIX_DTYPE)
              + _nbytes(w_a.shape, BF16) + _nbytes(w_b.shape, BF16) + 2 * _nbytes((tm, d), F32)
              + _nbytes((tm, d), BF16))

    def row(i):
        return (i, 0)

    def fixed(i):
        return (0, 0)

    return pl.pallas_call(
        _outproj_body,
        grid=(t // tm,),
        in_specs=[
            pl.BlockSpec((tm, HG_VAL), row),
            pl.BlockSpec((tm, SSD_WIDTH), row),
            pl.BlockSpec(w_a.shape, fixed),
            pl.BlockSpec(w_b.shape, fixed),
            pl.BlockSpec((tm, d), row),
            pl.BlockSpec((1, d), fixed),
            pl.BlockSpec((1, d), fixed),
        ],
        out_specs=[pl.BlockSpec((tm, d), row), pl.BlockSpec((tm, d), row)],
        out_shape=[jax.ShapeDtypeStruct((t, d), F32), jax.ShapeDtypeStruct((t, d), BF16)],
        compiler_params=pltpu.CompilerParams(
            dimension_semantics=("parallel",),
            vmem_limit_bytes=_vmem_limit(blocks),
        ),
        name="out_proj",
    )(o_a, o_b, w_a, w_b, x2, post_w, pre_w)


def _ffn_body(h_ref, x1c_ref, wg_ref, wu_ref, wd_ref, postw_ref, o_ref, x1_ref):
    j = pl.program_id(1)
    d = o_ref.shape[1]
    xc = x1c_ref.shape[1]

    @pl.when(j == 0)
    def _():
        o_ref[...] = jnp.zeros_like(o_ref)

    @pl.when(j < d // xc)
    def _():
        x1_ref[:, pl.ds(pl.multiple_of(j * xc, xc), xc)] = x1c_ref[...]

    h = h_ref[...]
    hids = []
    for s in range(FFN_TF // FFN_SUB):
        cs = slice(s * FFN_SUB, (s + 1) * FFN_SUB)
        g = _dot(h, wg_ref[:, cs])
        u = _dot(h, wu_ref[:, cs])
        hids.append((_silu(g) * u).astype(BF16))
    hid = jnp.concatenate(hids, axis=1)
    for c in range(d // FFN_DOWN_COLS):
        cs = slice(c * FFN_DOWN_COLS, (c + 1) * FFN_DOWN_COLS)
        o_ref[:, cs] += _dot(hid, wd_ref[:, cs])

    @pl.when(j == pl.num_programs(1) - 1)
    def _():
        o_ref[...] = x1_ref[...] + _rms(o_ref[...]) * postw_ref[...]


def _ffn(h2, x1, w_gate, w_up, w_down, post_w):
    t, d = x1.shape
    f = w_gate.shape[1]
    tm, tf, xc = FFN_TM, FFN_TF, FFN_X1_COLS
    n_xc = d // xc
    assert f // tf >= n_xc
    blocks = (_nbytes((tm, d), BF16) + _nbytes((tm, xc), F32) + 2 * _nbytes((d, tf), BF16)
              + _nbytes((tf, d), BF16) + _nbytes((tm, d), F32))
    values = (2 * _nbytes((tm, tf), F32) + _nbytes((tm, tf), BF16) + _nbytes((tm, d), F32))
    return pl.pallas_call(
        _ffn_body,
        grid=(t // tm, f // tf),
        in_specs=[
            pl.BlockSpec((tm, d), lambda i, j: (i, 0)),
            pl.BlockSpec((tm, xc), lambda i, j: (i, jnp.minimum(j, n_xc - 1))),
            pl.BlockSpec((d, tf), lambda i, j: (0, j)),
            pl.BlockSpec((d, tf), lambda i, j: (0, j)),
            pl.BlockSpec((tf, d), lambda i, j: (j, 0)),
            pl.BlockSpec((1, d), lambda i, j: (0, 0)),
        ],
        out_specs=pl.BlockSpec((tm, d), lambda i, j: (i, 0)),
        out_shape=jax.ShapeDtypeStruct((t, d), F32),
        scratch_shapes=[pltpu.VMEM((tm, d), F32)],
        compiler_params=pltpu.CompilerParams(
            dimension_semantics=("parallel", "arbitrary"),
            vmem_limit_bytes=_vmem_limit(blocks, _nbytes((tm, d), F32), values),
        ),
        name="ffn",
    )(h2, x1, w_gate, w_up, w_down, post_w)


def _pad_lanes(v):
    return jnp.pad(v.astype(F32), (0, V7X_LANES - v.shape[0]))[None, :]


def _arrange_w_in(w):
    parts = []
    for h in range(HG_HEADS):
        for blk in range(4):
            lo = blk * HG_KEY + h * HG_DK
            parts.append(w[:, lo:lo + HG_DK])
    ssd_lo = 2 * HG_KEY + 2 * HG_VAL
    n_main = ssd_lo + SSD_WIDTH + SSD_WIDTH + SSD_BC_WIDTH
    parts.append(w[:, ssd_lo:n_main])
    parts.append(jnp.pad(w[:, n_main:], ((0, 0), (0, V7X_LANES - SSD_HEADS))))
    return jnp.concatenate(parts, axis=1).astype(BF16)


def kernel(x, pre_mix_norm_w, w_in, lb_logits, conv_w, conv_b, dt_bias, a_log, d_skip, hgrn_norm_w,
           ssd_norm_w, w_out, post_mix_norm_w, pre_ffn_norm_w, w_gate, w_up, w_down, post_ffn_norm_w):
    batch, seqlen, d = x.shape
    depth = w_in.shape[0]
    expand = (jnp.arange(V7X_LANES)[:, None] == (jnp.arange(SSD_WIDTH)[None, :] // SSD_HEADDIM)).astype(BF16)

    x2 = x.reshape(batch * seqlen, d)
    for l in range(depth):
        o_a, o_b = _mixer(
            x2, pre_mix_norm_w[l][None, :], _arrange_w_in(w_in[l]), lb_logits.astype(F32),
            hgrn_norm_w[l][None, :], conv_w[l], conv_b[l][None, :], _pad_lanes(dt_bias[l]),
            _pad_lanes(a_log[l]), jnp.repeat(d_skip[l].astype(F32), SSD_HEADDIM)[None, :],
            ssd_norm_w[l][None, :], expand, seqlen, l)
        x1, h2 = _out_proj(o_a, o_b, w_out[l, :HG_VAL].astype(BF16), w_out[l, HG_VAL:].astype(BF16),
                           x2, post_mix_norm_w[l][None, :], pre_ffn_norm_w[l][None, :])
        x2 = _ffn(h2, x1, w_gate[l].astype(BF16), w_up[l].astype(BF16), w_down[l].astype(BF16),
                  post_ffn_norm_w[l][None, :])
    return x2.reshape(batch, seqlen, d)
```

```python
import functools

import jax
import jax.numpy as jnp
from jax import lax
from jax.experimental import pallas as pl
from jax.experimental.pallas import tpu as pltpu

F32 = jnp.float32
BF16 = jnp.bfloat16

HG_HEADS = 8
HG_DK = 128
HG_DV = 128
HG_KEY = HG_HEADS * HG_DK
HG_VAL = HG_HEADS * HG_DV
SSD_HEADS = 16
SSD_HEADDIM = 64
SSD_WIDTH = SSD_HEADS * SSD_HEADDIM
SSD_GROUPS = 2
SSD_HPG = SSD_HEADS // SSD_GROUPS
SSD_STATE = 128
SSD_CONV = 4
SSD_GROUP_WIDTH = SSD_HPG * SSD_HEADDIM
SSD_BC_WIDTH = 2 * SSD_GROUPS * SSD_STATE
NORM_EPS = 1e-6

V7X_LANES = 128
V7X_SUBLANES = 8
V7X_VMEM_BYTES = 64 * 1024 * 1024
V7X_VMEM_COMPILER_RESERVE = 6 * 1024 * 1024
V7X_VMEM_UNSCOPED = 2 * 1024 * 1024

MIX_DTYPE = BF16
MIX_CHUNK = 256
HGRN_HALF = 128
HGRN_HEADS_PER_GROUP = 4
MIX_PIECE_COLS = 256
HG_HEAD_COLS = 2 * HG_DK + 2 * HG_DV
SSD_COLS = SSD_WIDTH + SSD_WIDTH + SSD_BC_WIDTH + V7X_LANES
OUTPROJ_TM = 512
OUTPROJ_SUB = 256
FFN_TM = 1024
FFN_TF = 512
FFN_SUB = 256
FFN_DOWN_COLS = 512
FFN_X1_COLS = 256
NEG_BIG = -1e30
LOG2_E = 1.4426950408889634


def _nbytes(shape, dtype):
    n = 1
    for s in shape:
        n *= s
    return n * jnp.dtype(dtype).itemsize


def _vmem_limit(block_bytes, scratch_bytes=0, value_bytes=0):
    need = 2 * block_bytes + scratch_bytes + value_bytes + V7X_VMEM_COMPILER_RESERVE
    return int(min(need, V7X_VMEM_BYTES - V7X_VMEM_UNSCOPED))


def _sigmoid(x):
    return 1.0 / (1.0 + jnp.exp2(x * (-LOG2_E)))


def _silu(x):
    return x * _sigmoid(x)


def _rms(x):
    return x * lax.rsqrt(jnp.mean(x * x, axis=-1, keepdims=True) + NORM_EPS)


def _dot(a, b):
    return jnp.dot(a, b, preferred_element_type=F32)


def _dot_nt(a, b):
    return lax.dot_general(a, b, (((1,), (1,)), ((), ())), preferred_element_type=F32)


def _split2(x, p):
    x4 = x.reshape(x.shape[0] // (2 * p), 2, p, x.shape[1])
    return x4[:, 0], x4[:, 1]


def _merge2(lo, hi):
    return jnp.stack([lo, hi], axis=1).reshape(-1, lo.shape[-1])


def _hgrn_group(heads, lvl_ref, filler):
    g = len(heads)
    w = MIX_CHUNK
    hw = HGRN_HALF
    sub = V7X_SUBLANES
    rows = g * w
    n_half = rows // hw
    assert w == 2 * hw and hw == V7X_LANES
    shp3 = (rows // sub, sub, HG_DK)
    sub_idx = lax.broadcasted_iota(jnp.int32, (1, sub, HG_DK), 1)

    q = jnp.concatenate([_silu(h[0]) * (HG_DK ** -0.5) for h in heads], axis=0)
    f = jnp.concatenate([h[4] + (1.0 - h[4]) * _sigmoid(h[1]) for h in heads], axis=0)
    k = 1.0 - f

    def half_scores(qn, kn, level, a_blocks):
        qb, kb = qn.astype(BF16), kn.astype(BF16)
        out = []
        for c in range(n_half):
            rs = slice(c * hw, (c + 1) * hw)
            s_c = _dot_nt(qb[rs], kb[rs])
            keep = lvl_ref[...] == level
            out.append(jnp.where(keep, s_c, 0.0 if a_blocks is None else a_blocks[c]))
        return out

    a_diag = half_scores(q, k, 0, None)
    filler()
    cq = f.reshape(shp3)
    ck = jnp.ones(shp3, F32)
    r = cq
    q3, k3 = q.reshape(shp3), k.reshape(shp3)
    level = 1
    p = 1
    while p < sub:
        a_diag = half_scores((q3 * cq).reshape(rows, HG_DK), (k3 * ck).reshape(rows, HG_DK), level, a_diag)
        upper = (sub_idx & p) != 0
        down = pltpu.roll(r, p, axis=1)
        up = pltpu.roll(r, sub - p, axis=1)
        cq = cq * jnp.where(upper, down, 1.0)
        ck = ck * jnp.where(upper, 1.0, up)
        r = r * jnp.where(upper, down, up)
        filler()
        p *= 2
        level += 1
    cq = cq.reshape(rows, HG_DK)
    ck = ck.reshape(rows, HG_DK)

    a_cross = None
    while p < w:
        cq_lo, cq_hi = _split2(cq, p)
        ck_lo, ck_hi = _split2(ck, p)
        q_lo, q_hi = _split2(q, p)
        k_lo, k_hi = _split2(k, p)
        nb = rows // (2 * p)
        if p < hw:
            qu = (q_hi * cq_hi).reshape(rows // 2, HG_DK).astype(BF16)
            kn = _merge2(k_lo * ck_lo, k_hi).astype(BF16)
            new_a = []
            for c in range(n_half):
                s_c = _dot_nt(qu[c * (hw // 2):(c + 1) * (hw // 2)], kn[c * hw:(c + 1) * hw])
                a_lo, a_hi = _split2(a_diag[c], p)
                _, lv_hi = _split2(lvl_ref[...], p)
                a_hi = jnp.where(lv_hi == level, s_c.reshape(a_hi.shape), a_hi)
                new_a.append(_merge2(a_lo, a_hi))
            a_diag = new_a
        else:
            qu = (q_hi * cq_hi).astype(BF16)
            kl = (k_lo * ck_lo).astype(BF16)
            a_cross = [_dot_nt(qu[i], kl[i]) for i in range(g)]
        r4 = r.reshape(nb, 2, sub, HG_DK)
        r_lo, r_hi = r4[:, 0], r4[:, 1]
        cq_hi = (cq_hi.reshape(nb, p // sub, sub, HG_DK) * r_lo[:, None]).reshape(nb, p, HG_DK)
        ck_lo = (ck_lo.reshape(nb, p // sub, sub, HG_DK) * r_hi[:, None]).reshape(nb, p, HG_DK)
        cq = _merge2(cq_lo, cq_hi)
        ck = _merge2(ck_lo, ck_hi)
        r = r_lo * r_hi
        filler()
        p *= 2
        level += 1

    qn = (q * cq).astype(BF16)
    kn = (k * ck).astype(BF16)
    results = []
    for i, (_, _, v, gate, _, nw, st) in enumerate(heads):
        rs = slice(i * w, (i + 1) * w)
        vb = v.astype(BF16)
        o_inter = _dot_nt(qn[rs], st.astype(BF16))
        o0 = _dot(a_diag[2 * i].astype(BF16), vb[:hw])
        o1 = _dot(jnp.concatenate([a_cross[i], a_diag[2 * i + 1]], axis=1).astype(BF16), vb)
        o = jnp.concatenate([o0, o1], axis=0) + o_inter
        st_new = st * r[i, 0:1, :] + _dot(v.T.astype(BF16), kn[rs])
        results.append(((_rms(o) * nw * _silu(gate)).astype(MIX_DTYPE), st_new))
    return results


def _split3(x):
    hi = x.astype(BF16)
    r1 = x - hi.astype(F32)
    mid = r1.astype(BF16)
    lo = (r1 - mid.astype(F32)).astype(BF16)
    return hi, mid, lo


def _dot_exact01_left(m01, x):
    hi, mid, lo = _split3(x)
    return _dot(m01, hi) + _dot(m01, mid) + _dot(m01, lo)


def _dot_expand(x, m01):
    hi = x.astype(BF16)
    mid = (x - hi.astype(F32)).astype(BF16)
    return _dot(hi, m01) + _dot(mid, m01)


def _ssd_chunk(ps, cw_ref, cb_ref, dtb_ref, alog_ref, dskip_ref, nw_ref, expand_ref, o_ref,
               xbuf_ref, state_ref, filler):
    wc = MIX_CHUNK
    hdr = V7X_SUBLANES
    gw = SSD_GROUP_WIDTH
    z = ps[:, :SSD_WIDTH]
    xbc = ps[:, SSD_WIDTH:SSD_WIDTH + SSD_WIDTH + SSD_BC_WIDTH]
    dt_raw = ps[:, SSD_WIDTH + SSD_WIDTH + SSD_BC_WIDTH:]

    n_slabs = xbc.shape[1] // V7X_LANES
    cols = []
    for si in range(n_slabs):
        cs = slice(si * V7X_LANES, (si + 1) * V7X_LANES)
        xbuf_ref[si, hdr:hdr + wc, :] = xbc[:, cs]
        acc = cb_ref[:, cs]
        for kk in range(SSD_CONV):
            off = hdr - (SSD_CONV - 1) + kk
            acc = acc + cw_ref[kk:kk + 1, cs] * xbuf_ref[si, off:off + wc, :]
        xbuf_ref[si, 0:hdr, :] = xbuf_ref[si, wc:wc + hdr, :]
        cols.append(_silu(acc))
    per_group = gw // V7X_LANES
    xs = [jnp.concatenate(cols[g * per_group:(g + 1) * per_group], axis=1) for g in range(SSD_GROUPS)]
    bcs = jnp.concatenate(cols[SSD_GROUPS * per_group:], axis=1)

    dtr = dt_raw + dtb_ref[...]
    dt = jnp.maximum(dtr, 0.0) + jnp.log1p(jnp.exp(-jnp.abs(dtr)))
    a = dt * (-jnp.exp(alog_ref[...]))
    t_idx = lax.broadcasted_iota(jnp.int32, (wc, wc), 0)
    s_idx = lax.broadcasted_iota(jnp.int32, (wc, wc), 1)
    causal = t_idx >= s_idx
    tril01 = jnp.where(causal, 1.0, 0.0).astype(BF16)
    acs = _dot_exact01_left(tril01, a) * LOG2_E
    acs_t = acs.T

    expand = expand_ref[...]
    acs_x = _dot_expand(acs, expand)
    dt_x = _dot_expand(dt, expand)
    lane = lax.broadcasted_iota(jnp.int32, (wc, V7X_LANES), 1)
    first_half = lane < SSD_HEADDIM

    for g in range(SSD_GROUPS):
        gs = slice(g * gw, (g + 1) * gw)
        xs_g = xs[g]
        xdt = xs_g * dt_x[:, gs]
        xdt_b = xdt.astype(BF16)
        b_g = bcs[:, g * SSD_STATE:(g + 1) * SSD_STATE]
        c_g = bcs[:, (SSD_GROUPS + g) * SSD_STATE:(SSD_GROUPS + g + 1) * SSD_STATE]
        c_b = c_g.astype(BF16)
        cb = _dot_nt(c_b, b_g.astype(BF16))

        pieces = []
        for j in range(SSD_HPG // 2):
            filler()
            xp = xdt_b[:, j * V7X_LANES:(j + 1) * V7X_LANES]
            acc = None
            for half in range(2):
                h = g * SSD_HPG + 2 * j + half
                seg = acs[:, h:h + 1] - acs_t[h:h + 1, :]
                m = (cb * jnp.exp2(jnp.where(causal, seg, NEG_BIG))).astype(BF16)
                keep = first_half if half == 0 else jnp.logical_not(first_half)
                part = _dot(m, jnp.where(keep, xp, jnp.zeros_like(xp)))
                acc = part if acc is None else acc + part
            pieces.append(acc)
        y_diag = jnp.concatenate(pieces, axis=-1)

        st = state_ref[g]
        acs_g = acs_x[:, gs]
        y_off = _dot(c_b, st.astype(BF16)) * jnp.exp2(acs_g)
        y = y_diag + y_off + dskip_ref[:, gs] * xs_g

        last = acs_g[wc - 1:wc, :]
        xdec = (xdt * jnp.exp2(last - acs_g)).astype(BF16)
        state_ref[g] = st * jnp.exp2(last) + _dot(b_g.T.astype(BF16), xdec)

        yz = y * _silu(z[:, gs])
        o_ref[:, gs] = (_rms(yz) * nw_ref[:, gs]).astype(o_ref.dtype)


def _mixer_body(x_ref, prew_ref, w_ref, lbl_ref, hnw_ref, cw_ref, cb_ref, dtb_ref, alog_ref,
                dskip_ref, snw_ref, expand_ref, wg_ref, wu_ref, oa_ref, ob_ref, wgo_ref, wuo_ref,
                hst_ref, lvl_ref, lb_ref, xbuf_ref, sst_ref, pj_ref, *, layer, chunks_per_seq):
    hw = HGRN_HALF
    s = pl.program_id(0)
    grp = HGRN_HEADS_PER_GROUP
    ssd_lo = HG_HEADS * HG_HEAD_COLS
    wgo_ref[...] = wg_ref[...].astype(wgo_ref.dtype)
    wuo_ref[...] = wu_ref[...].astype(wuo_ref.dtype)
    n_all = w_ref.shape[1]

    @pl.when(s == 0)
    def _():
        pj_ref[...] = jnp.zeros_like(pj_ref)
        t_idx = lax.broadcasted_iota(jnp.int32, (hw, hw), 0)
        s_idx = lax.broadcasted_iota(jnp.int32, (hw, hw), 1)
        lv = 32 - lax.clz(t_idx ^ s_idx)
        lvl_ref[...] = jnp.where(t_idx >= s_idx, lv, -1)
        lg = lbl_ref[...]
        e = jnp.exp(lg - jnp.max(lg, axis=0, keepdims=True))
        sm = e / jnp.sum(e, axis=0, keepdims=True)
        lb_ref[...] = jnp.sum(sm[: layer + 1], axis=0, keepdims=True)

    @pl.when(jnp.logical_or(s == 0, lax.rem(s - 1, chunks_per_seq) == 0))
    def _():
        hst_ref[...] = jnp.zeros_like(hst_ref)
        sst_ref[...] = jnp.zeros_like(sst_ref)
        xbuf_ref[:, 0:V7X_SUBLANES, :] = jnp.zeros((xbuf_ref.shape[0], V7X_SUBLANES, V7X_LANES), F32)

    hb = (_rms(x_ref[...]) * prew_ref[...]).astype(BF16)
    todo = [(lo, min(MIX_PIECE_COLS, n_all - lo)) for lo in range(0, n_all, MIX_PIECE_COLS)]
    new_proj = []

    def filler():
        if todo:
            lo, width = todo.pop(0)
            new_proj.append((lo, width, _dot(hb, w_ref[:, lo:lo + width])))

    _ssd_chunk(pj_ref[:, ssd_lo:], cw_ref, cb_ref, dtb_ref, alog_ref, dskip_ref, snw_ref, expand_ref,
               ob_ref, xbuf_ref, sst_ref, filler)
    for gi in range(HG_HEADS // grp):
        heads = []
        for g in range(grp):
            h = gi * grp + g
            base = h * HG_HEAD_COLS
            ks = slice(h * HG_DK, (h + 1) * HG_DK)
            vs = slice(h * HG_DV, (h + 1) * HG_DV)
            heads.append((pj_ref[:, base:base + HG_DK], pj_ref[:, base + HG_DK:base + 2 * HG_DK],
                          pj_ref[:, base + 2 * HG_DK:base + 2 * HG_DK + HG_DV],
                          pj_ref[:, base + 2 * HG_DK + HG_DV:base + HG_HEAD_COLS],
                          lb_ref[:, ks], hnw_ref[:, vs], hst_ref[h]))
        for g, (out, st_new) in enumerate(_hgrn_group(heads, lvl_ref, filler)):
            h = gi * grp + g
            hst_ref[h] = st_new
            oa_ref[:, h * HG_DV:(h + 1) * HG_DV] = out
    while todo:
        filler()

    for lo, width, val in new_proj:
        pj_ref[:, lo:lo + width] = val


def _mixer(x2, pre_w, w_all, lb_logits, hgrn_nw, conv_w, conv_b, dt_bias_p, a_log_p, d_skip_x,
           ssd_nw, expand, w_gate, w_up, seqlen, layer):
    t, d = x2.shape
    wc = MIX_CHUNK
    n_chunks = t // wc
    f = w_gate.shape[1]
    wf_rows = d // n_chunks
    assert wf_rows * n_chunks == d and wf_rows % (2 * V7X_SUBLANES) == 0
    n_all = w_all.shape[1]
    assert n_all == HG_HEADS * HG_HEAD_COLS + SSD_COLS
    xbuf_shape = ((SSD_WIDTH + SSD_BC_WIDTH) // V7X_LANES, V7X_SUBLANES + wc, V7X_LANES)
    blocks = (_nbytes((wc, d), F32) + _nbytes((wc, HG_VAL), MIX_DTYPE) + _nbytes((wc, SSD_WIDTH), MIX_DTYPE)
              + 2 * _nbytes((wf_rows, f), F32) + 2 * _nbytes((wf_rows, f), BF16))
    resident = _nbytes(w_all.shape, BF16) + 2 * _nbytes(expand.shape, BF16)
    scratch = (_nbytes((HG_HEADS, HG_DV, HG_DK), F32) + _nbytes((HGRN_HALF, HGRN_HALF), jnp.int32)
               + _nbytes((V7X_SUBLANES, HG_KEY), F32) + _nbytes(xbuf_shape, F32)
               + _nbytes((SSD_GROUPS, SSD_STATE, SSD_GROUP_WIDTH), F32) + _nbytes((wc, n_all), F32))
    values = _nbytes((wc, d), BF16) + _nbytes((wc, n_all), F32) + 8 * _nbytes((wc, SSD_WIDTH), F32)

    def whole(arr, **kw):
        return pl.BlockSpec(arr.shape, lambda s: (0,) * arr.ndim, **kw)

    def finished(width):
        return pl.BlockSpec((wc, width), lambda s: (jnp.maximum(s - 1, 0), 0))

    wf_spec = pl.BlockSpec((wf_rows, f), lambda s: (jnp.minimum(s, n_chunks - 1), 0))

    return pl.pallas_call(
        functools.partial(_mixer_body, layer=layer, chunks_per_seq=seqlen // wc),
        grid=(n_chunks + 1,),
        in_specs=[
            pl.BlockSpec((wc, d), lambda s: (jnp.minimum(s, n_chunks - 1), 0)),
            whole(pre_w),
            whole(w_all, pipeline_mode=pl.Buffered(1)),
            whole(lb_logits), whole(hgrn_nw), whole(conv_w), whole(conv_b), whole(dt_bias_p),
            whole(a_log_p), whole(d_skip_x), whole(ssd_nw), whole(expand), wf_spec, wf_spec,
        ],
        out_specs=[finished(HG_VAL), finished(SSD_WIDTH), wf_spec, wf_spec],
        out_shape=[jax.ShapeDtypeStruct((t, HG_VAL), MIX_DTYPE),
                   jax.ShapeDtypeStruct((t, SSD_WIDTH), MIX_DTYPE),
                   jax.ShapeDtypeStruct(w_gate.shape, BF16), jax.ShapeDtypeStruct(w_up.shape, BF16)],
        scratch_shapes=[
            pltpu.VMEM((HG_HEADS, HG_DV, HG_DK), F32),
            pltpu.VMEM((HGRN_HALF, HGRN_HALF), jnp.int32),
            pltpu.VMEM((1, HG_KEY), F32),
            pltpu.VMEM(xbuf_shape, F32),
            pltpu.VMEM((SSD_GROUPS, SSD_STATE, SSD_GROUP_WIDTH), F32),
            pltpu.VMEM((wc, n_all), F32),
        ],
        compiler_params=pltpu.CompilerParams(
            dimension_semantics=("arbitrary",),
            vmem_limit_bytes=_vmem_limit(blocks, resident + scratch, values),
        ),
        name="mixer",
    )(x2, pre_w, w_all, lb_logits, hgrn_nw, conv_w, conv_b, dt_bias_p, a_log_p, d_skip_x, ssd_nw, expand,
      w_gate, w_up)


def _outproj_body(oa_ref, ob_ref, wa_ref, wb_ref, x_ref, postw_ref, prew_ref, wd_ref,
                  x1_ref, h2_ref, wdo_ref):
    wdo_ref[...] = wd_ref[...].astype(wdo_ref.dtype)
    for r in range(x_ref.shape[0] // OUTPROJ_SUB):
        rs = slice(r * OUTPROJ_SUB, (r + 1) * OUTPROJ_SUB)
        mix = _dot(oa_ref[rs, :], wa_ref[...]) + _dot(ob_ref[rs, :], wb_ref[...])
        x1 = x_ref[rs, :] + _rms(mix) * postw_ref[...]
        x1_ref[rs, :] = x1
        h2_ref[rs, :] = (_rms(x1) * prew_ref[...]).astype(h2_ref.dtype)


def _out_proj(o_a, o_b, w_a, w_b, x2, post_w, pre_w, w_down):
    t, d = x2.shape
    tm = OUTPROJ_TM
    wd_rows = w_down.shape[0] // (t // tm)
    assert wd_rows * (t // tm) == w_down.shape[0] and wd_rows % (2 * V7X_SUBLANES) == 0
    blocks = (_nbytes((tm, HG_VAL), MIX_DTYPE) + _nbytes((tm, SSD_WIDTH), MIX_DTYPE)
              + _nbytes(w_a.shape, BF16) + _nbytes(w_b.shape, BF16) + 2 * _nbytes((tm, d), F32)
              + _nbytes((tm, d), BF16) + _nbytes((wd_rows, d), F32) + _nbytes((wd_rows, d), BF16))

    def row(i):
        return (i, 0)

    def fixed(i):
        return (0, 0)

    return pl.pallas_call(
        _outproj_body,
        grid=(t // tm,),
        in_specs=[
            pl.BlockSpec((tm, HG_VAL), row),
            pl.BlockSpec((tm, SSD_WIDTH), row),
            pl.BlockSpec(w_a.shape, fixed),
            pl.BlockSpec(w_b.shape, fixed),
            pl.BlockSpec((tm, d), row),
            pl.BlockSpec((1, d), fixed),
            pl.BlockSpec((1, d), fixed),
            pl.BlockSpec((wd_rows, d), row),
        ],
        out_specs=[pl.BlockSpec((tm, d), row), pl.BlockSpec((tm, d), row),
                   pl.BlockSpec((wd_rows, d), row)],
        out_shape=[jax.ShapeDtypeStruct((t, d), F32), jax.ShapeDtypeStruct((t, d), BF16),
                   jax.ShapeDtypeStruct(w_down.shape, BF16)],
        compiler_params=pltpu.CompilerParams(
            dimension_semantics=("parallel",),
            vmem_limit_bytes=_vmem_limit(blocks),
        ),
        name="out_proj",
    )(o_a, o_b, w_a, w_b, x2, post_w, pre_w, w_down)


def _ffn_body(h_ref, x1c_ref, wg_ref, wu_ref, wd_ref, postw_ref, o_ref, x1_ref):
    j = pl.program_id(1)
    d = o_ref.shape[1]
    xc = x1c_ref.shape[1]

    @pl.when(j == 0)
    def _():
        o_ref[...] = jnp.zeros_like(o_ref)

    @pl.when(j < d // xc)
    def _():
        x1_ref[:, pl.ds(pl.multiple_of(j * xc, xc), xc)] = x1c_ref[...]

    h = h_ref[...]
    hids = []
    for s in range(FFN_TF // FFN_SUB):
        cs = slice(s * FFN_SUB, (s + 1) * FFN_SUB)
        g = _dot(h, wg_ref[:, cs])
        u = _dot(h, wu_ref[:, cs])
        hids.append((_silu(g) * u).astype(BF16))
    hid = jnp.concatenate(hids, axis=1)
    for c in range(d // FFN_DOWN_COLS):
        cs = slice(c * FFN_DOWN_COLS, (c + 1) * FFN_DOWN_COLS)
        o_ref[:, cs] += _dot(hid, wd_ref[:, cs])

    @pl.when(j == pl.num_programs(1) - 1)
    def _():
        o_ref[...] = x1_ref[...] + _rms(o_ref[...]) * postw_ref[...]


def _ffn(h2, x1, w_gate, w_up, w_down, post_w):
    t, d = x1.shape
    f = w_gate.shape[1]
    tm, tf, xc = FFN_TM, FFN_TF, FFN_X1_COLS
    n_xc = d // xc
    assert f // tf >= n_xc
    blocks = (_nbytes((tm, d), BF16) + _nbytes((tm, xc), F32) + 2 * _nbytes((d, tf), BF16)
              + _nbytes((tf, d), BF16) + _nbytes((tm, d), F32))
    values = (2 * _nbytes((tm, tf), F32) + _nbytes((tm, tf), BF16) + _nbytes((tm, d), F32))
    return pl.pallas_call(
        _ffn_body,
        grid=(t // tm, f // tf),
        in_specs=[
            pl.BlockSpec((tm, d), lambda i, j: (i, 0)),
            pl.BlockSpec((tm, xc), lambda i, j: (i, jnp.minimum(j, n_xc - 1))),
            pl.BlockSpec((d, tf), lambda i, j: (0, j)),
            pl.BlockSpec((d, tf), lambda i, j: (0, j)),
            pl.BlockSpec((tf, d), lambda i, j: (j, 0)),
            pl.BlockSpec((1, d), lambda i, j: (0, 0)),
        ],
        out_specs=pl.BlockSpec((tm, d), lambda i, j: (i, 0)),
        out_shape=jax.ShapeDtypeStruct((t, d), F32),
        scratch_shapes=[pltpu.VMEM((tm, d), F32)],
        compiler_params=pltpu.CompilerParams(
            dimension_semantics=("parallel", "arbitrary"),
            vmem_limit_bytes=_vmem_limit(blocks, _nbytes((tm, d), F32), values),
        ),
        name="ffn",
    )(h2, x1, w_gate, w_up, w_down, post_w)


def _pad_lanes(v):
    return jnp.pad(v.astype(F32), (0, V7X_LANES - v.shape[0]))[None, :]


def _arrange_w_in(w):
    parts = []
    for h in range(HG_HEADS):
        for blk in range(4):
            lo = blk * HG_KEY + h * HG_DK
            parts.append(w[:, lo:lo + HG_DK])
    ssd_lo = 2 * HG_KEY + 2 * HG_VAL
    n_main = ssd_lo + SSD_WIDTH + SSD_WIDTH + SSD_BC_WIDTH
    parts.append(w[:, ssd_lo:n_main])
    parts.append(jnp.pad(w[:, n_main:], ((0, 0), (0, V7X_LANES - SSD_HEADS))))
    return jnp.concatenate(parts, axis=1).astype(BF16)


def kernel(x, pre_mix_norm_w, w_in, lb_logits, conv_w, conv_b, dt_bias, a_log, d_skip, hgrn_norm_w,
           ssd_norm_w, w_out, post_mix_norm_w, pre_ffn_norm_w, w_gate, w_up, w_down, post_ffn_norm_w):
    batch, seqlen, d = x.shape
    depth = w_in.shape[0]
    expand = (jnp.arange(V7X_LANES)[:, None] == (jnp.arange(SSD_WIDTH)[None, :] // SSD_HEADDIM)).astype(BF16)

    x2 = x.reshape(batch * seqlen, d)
    for l in range(depth):
        o_a, o_b, wg_b, wu_b = _mixer(
            x2, pre_mix_norm_w[l][None, :], _arrange_w_in(w_in[l]), lb_logits.astype(F32),
            hgrn_norm_w[l][None, :], conv_w[l], conv_b[l][None, :], _pad_lanes(dt_bias[l]),
            _pad_lanes(a_log[l]), jnp.repeat(d_skip[l].astype(F32), SSD_HEADDIM)[None, :],
            ssd_norm_w[l][None, :], expand, w_gate[l], w_up[l], seqlen, l)
        x1, h2, wd_b = _out_proj(o_a, o_b, w_out[l, :HG_VAL].astype(BF16), w_out[l, HG_VAL:].astype(BF16),
                                 x2, post_mix_norm_w[l][None, :], pre_ffn_norm_w[l][None, :], w_down[l])
        x2 = _ffn(h2, x1, wg_b, wu_b, wd_b, post_ffn_norm_w[l][None, :])
    return x2.reshape(batch, seqlen, d)
```

```python
import functools

import jax
import jax.numpy as jnp
from jax import lax
from jax.experimental import pallas as pl
from jax.experimental.pallas import tpu as pltpu

F32 = jnp.float32
BF16 = jnp.bfloat16

HG_HEADS = 8
HG_DK = 128
HG_DV = 128
HG_KEY = HG_HEADS * HG_DK
HG_VAL = HG_HEADS * HG_DV
SSD_HEADS = 16
SSD_HEADDIM = 64
SSD_WIDTH = SSD_HEADS * SSD_HEADDIM
SSD_GROUPS = 2
SSD_HPG = SSD_HEADS // SSD_GROUPS
SSD_STATE = 128
SSD_CONV = 4
SSD_GROUP_WIDTH = SSD_HPG * SSD_HEADDIM
SSD_BC_WIDTH = 2 * SSD_GROUPS * SSD_STATE
NORM_EPS = 1e-6

V7X_LANES = 128
V7X_SUBLANES = 8
V7X_VMEM_BYTES = 64 * 1024 * 1024
V7X_VMEM_COMPILER_RESERVE = 6 * 1024 * 1024
V7X_VMEM_UNSCOPED = 2 * 1024 * 1024

MIX_DTYPE = BF16
MIX_CHUNK = 256
HGRN_HALF = 128
HGRN_HEADS_PER_GROUP = 4
MIX_PIECE_COLS = 256
HG_HEAD_COLS = 2 * HG_DK + 2 * HG_DV
SSD_COLS = SSD_WIDTH + SSD_WIDTH + SSD_BC_WIDTH + V7X_LANES
OUTPROJ_TM = 512
OUTPROJ_SUB = 256
FFN_TM = 1024
FFN_TF = 512
FFN_SUB = 256
FFN_DOWN_COLS = 512
FFN_X1_COLS = 256
NEG_BIG = -1e30
LOG2_E = 1.4426950408889634


def _nbytes(shape, dtype):
    n = 1
    for s in shape:
        n *= s
    return n * jnp.dtype(dtype).itemsize


def _vmem_limit(block_bytes, scratch_bytes=0, value_bytes=0):
    need = 2 * block_bytes + scratch_bytes + value_bytes + V7X_VMEM_COMPILER_RESERVE
    return int(min(need, V7X_VMEM_BYTES - V7X_VMEM_UNSCOPED))


def _sigmoid(x):
    return 1.0 / (1.0 + jnp.exp2(x * (-LOG2_E)))


def _silu(x):
    return x * _sigmoid(x)


def _rms(x):
    return x * lax.rsqrt(jnp.mean(x * x, axis=-1, keepdims=True) + NORM_EPS)


def _dot(a, b):
    return jnp.dot(a, b, preferred_element_type=F32)


def _dot_nt(a, b):
    return lax.dot_general(a, b, (((1,), (1,)), ((), ())), preferred_element_type=F32)


def _split2(x, p):
    x4 = x.reshape(x.shape[0] // (2 * p), 2, p, x.shape[1])
    return x4[:, 0], x4[:, 1]


def _merge2(lo, hi):
    return jnp.stack([lo, hi], axis=1).reshape(-1, lo.shape[-1])


def _hgrn_group(heads, lvl_ref, filler):
    g = len(heads)
    w = MIX_CHUNK
    hw = HGRN_HALF
    sub = V7X_SUBLANES
    rows = g * w
    n_half = rows // hw
    assert w == 2 * hw and hw == V7X_LANES
    shp3 = (rows // sub, sub, HG_DK)
    sub_idx = lax.broadcasted_iota(jnp.int32, (1, sub, HG_DK), 1)

    q = jnp.concatenate([_silu(h[0]) * (HG_DK ** -0.5) for h in heads], axis=0)
    f = jnp.concatenate([h[4] + (1.0 - h[4]) * _sigmoid(h[1]) for h in heads], axis=0)
    k = 1.0 - f

    def half_scores(qn, kn, level, a_blocks):
        qb, kb = qn.astype(BF16), kn.astype(BF16)
        out = []
        for c in range(n_half):
            rs = slice(c * hw, (c + 1) * hw)
            s_c = _dot_nt(qb[rs], kb[rs])
            keep = lvl_ref[...] == level
            out.append(jnp.where(keep, s_c, 0.0 if a_blocks is None else a_blocks[c]))
        return out

    a_diag = half_scores(q, k, 0, None)
    filler()
    cq = f.reshape(shp3)
    ck = jnp.ones(shp3, F32)
    r = cq
    q3, k3 = q.reshape(shp3), k.reshape(shp3)
    level = 1
    p = 1
    while p < sub:
        a_diag = half_scores((q3 * cq).reshape(rows, HG_DK), (k3 * ck).reshape(rows, HG_DK), level, a_diag)
        upper = (sub_idx & p) != 0
        down = pltpu.roll(r, p, axis=1)
        up = pltpu.roll(r, sub - p, axis=1)
        cq = cq * jnp.where(upper, down, 1.0)
        ck = ck * jnp.where(upper, 1.0, up)
        r = r * jnp.where(upper, down, up)
        filler()
        p *= 2
        level += 1
    cq = cq.reshape(rows, HG_DK)
    ck = ck.reshape(rows, HG_DK)

    a_cross = None
    while p < w:
        cq_lo, cq_hi = _split2(cq, p)
        ck_lo, ck_hi = _split2(ck, p)
        q_lo, q_hi = _split2(q, p)
        k_lo, k_hi = _split2(k, p)
        nb = rows // (2 * p)
        if p < hw:
            qu = (q_hi * cq_hi).reshape(rows // 2, HG_DK).astype(BF16)
            kn = _merge2(k_lo * ck_lo, k_hi).astype(BF16)
            new_a = []
            for c in range(n_half):
                s_c = _dot_nt(qu[c * (hw // 2):(c + 1) * (hw // 2)], kn[c * hw:(c + 1) * hw])
                a_lo, a_hi = _split2(a_diag[c], p)
                _, lv_hi = _split2(lvl_ref[...], p)
                a_hi = jnp.where(lv_hi == level, s_c.reshape(a_hi.shape), a_hi)
                new_a.append(_merge2(a_lo, a_hi))
            a_diag = new_a
        else:
            qu = (q_hi * cq_hi).astype(BF16)
            kl = (k_lo * ck_lo).astype(BF16)
            a_cross = [_dot_nt(qu[i], kl[i]) for i in range(g)]
        r4 = r.reshape(nb, 2, sub, HG_DK)
        r_lo, r_hi = r4[:, 0], r4[:, 1]
        cq_hi = (cq_hi.reshape(nb, p // sub, sub, HG_DK) * r_lo[:, None]).reshape(nb, p, HG_DK)
        ck_lo = (ck_lo.reshape(nb, p // sub, sub, HG_DK) * r_hi[:, None]).reshape(nb, p, HG_DK)
        cq = _merge2(cq_lo, cq_hi)
        ck = _merge2(ck_lo, ck_hi)
        r = r_lo * r_hi
        filler()
        p *= 2
        level += 1

    qn = (q * cq).astype(BF16)
    kn = (k * ck).astype(BF16)
    results = []
    for i, (_, _, v, gate, _, nw, st) in enumerate(heads):
        rs = slice(i * w, (i + 1) * w)
        vb = v.astype(BF16)
        o_inter = _dot_nt(qn[rs], st.astype(BF16))
        o0 = _dot(a_diag[2 * i].astype(BF16), vb[:hw])
        o1 = _dot(jnp.concatenate([a_cross[i], a_diag[2 * i + 1]], axis=1).astype(BF16), vb)
        o = jnp.concatenate([o0, o1], axis=0) + o_inter
        st_new = st * r[i, 0:1, :] + _dot(v.T.astype(BF16), kn[rs])
        results.append(((_rms(o) * nw * _silu(gate)).astype(MIX_DTYPE), st_new))
    return results


def _split3(x):
    hi = x.astype(BF16)
    r1 = x - hi.astype(F32)
    mid = r1.astype(BF16)
    lo = (r1 - mid.astype(F32)).astype(BF16)
    return hi, mid, lo


def _dot_exact01_left(m01, x):
    hi, mid, lo = _split3(x)
    return _dot(m01, hi) + _dot(m01, mid) + _dot(m01, lo)


def _dot_expand(x, m01):
    hi = x.astype(BF16)
    mid = (x - hi.astype(F32)).astype(BF16)
    return _dot(hi, m01) + _dot(mid, m01)


def _ssd_chunk(ps, cw_ref, cb_ref, dtb_ref, alog_ref, dskip_ref, nw_ref, expand_ref, o_ref,
               xbuf_ref, state_ref, filler):
    wc = MIX_CHUNK
    hdr = V7X_SUBLANES
    gw = SSD_GROUP_WIDTH
    z = ps[:, :SSD_WIDTH]
    xbc = ps[:, SSD_WIDTH:SSD_WIDTH + SSD_WIDTH + SSD_BC_WIDTH]
    dt_raw = ps[:, SSD_WIDTH + SSD_WIDTH + SSD_BC_WIDTH:]

    n_slabs = xbc.shape[1] // V7X_LANES
    cols = []
    for si in range(n_slabs):
        cs = slice(si * V7X_LANES, (si + 1) * V7X_LANES)
        xbuf_ref[si, hdr:hdr + wc, :] = xbc[:, cs]
        acc = cb_ref[:, cs]
        for kk in range(SSD_CONV):
            off = hdr - (SSD_CONV - 1) + kk
            acc = acc + cw_ref[kk:kk + 1, cs] * xbuf_ref[si, off:off + wc, :]
        xbuf_ref[si, 0:hdr, :] = xbuf_ref[si, wc:wc + hdr, :]
        cols.append(_silu(acc))
    per_group = gw // V7X_LANES
    xs = [jnp.concatenate(cols[g * per_group:(g + 1) * per_group], axis=1) for g in range(SSD_GROUPS)]
    bcs = jnp.concatenate(cols[SSD_GROUPS * per_group:], axis=1)

    dtr = dt_raw + dtb_ref[...]
    dt = jnp.maximum(dtr, 0.0) + jnp.log1p(jnp.exp(-jnp.abs(dtr)))
    a = dt * (-jnp.exp(alog_ref[...]))
    t_idx = lax.broadcasted_iota(jnp.int32, (wc, wc), 0)
    s_idx = lax.broadcasted_iota(jnp.int32, (wc, wc), 1)
    causal = t_idx >= s_idx
    tril01 = jnp.where(causal, 1.0, 0.0).astype(BF16)
    acs = _dot_exact01_left(tril01, a) * LOG2_E
    acs_t = acs.T

    expand = expand_ref[...]
    acs_x = _dot_expand(acs, expand)
    dt_x = _dot_expand(dt, expand)
    lane = lax.broadcasted_iota(jnp.int32, (wc, V7X_LANES), 1)
    first_half = lane < SSD_HEADDIM

    for g in range(SSD_GROUPS):
        gs = slice(g * gw, (g + 1) * gw)
        xs_g = xs[g]
        xdt = xs_g * dt_x[:, gs]
        xdt_b = xdt.astype(BF16)
        b_g = bcs[:, g * SSD_STATE:(g + 1) * SSD_STATE]
        c_g = bcs[:, (SSD_GROUPS + g) * SSD_STATE:(SSD_GROUPS + g + 1) * SSD_STATE]
        c_b = c_g.astype(BF16)
        cb = _dot_nt(c_b, b_g.astype(BF16))

        pieces = []
        for j in range(SSD_HPG // 2):
            filler()
            xp = xdt_b[:, j * V7X_LANES:(j + 1) * V7X_LANES]
            acc = None
            for half in range(2):
                h = g * SSD_HPG + 2 * j + half
                seg = acs[:, h:h + 1] - acs_t[h:h + 1, :]
                m = (cb * jnp.exp2(jnp.where(causal, seg, NEG_BIG))).astype(BF16)
                keep = first_half if half == 0 else jnp.logical_not(first_half)
                part = _dot(m, jnp.where(keep, xp, jnp.zeros_like(xp)))
                acc = part if acc is None else acc + part
            pieces.append(acc)
        y_diag = jnp.concatenate(pieces, axis=-1)

        st = state_ref[g]
        acs_g = acs_x[:, gs]
        y_off = _dot(c_b, st.astype(BF16)) * jnp.exp2(acs_g)
        y = y_diag + y_off + dskip_ref[:, gs] * xs_g

        last = acs_g[wc - 1:wc, :]
        xdec = (xdt * jnp.exp2(last - acs_g)).astype(BF16)
        state_ref[g] = st * jnp.exp2(last) + _dot(b_g.T.astype(BF16), xdec)

        yz = y * _silu(z[:, gs])
        o_ref[:, gs] = (_rms(yz) * nw_ref[:, gs]).astype(o_ref.dtype)


def _mixer_body(x_ref, prew_ref, w_ref, lbl_ref, hnw_ref, cw_ref, cb_ref, dtb_ref, alog_ref,
                dskip_ref, snw_ref, expand_ref, wg_ref, wu_ref, oa_ref, ob_ref, wgo_ref, wuo_ref,
                hst_ref, lvl_ref, lb_ref, xbuf_ref, sst_ref, pj_ref, *, layer, chunks_per_seq):
    hw = HGRN_HALF
    s = pl.program_id(0)
    grp = HGRN_HEADS_PER_GROUP
    ssd_lo = HG_HEADS * HG_HEAD_COLS
    wgo_ref[...] = wg_ref[...].astype(wgo_ref.dtype)
    wuo_ref[...] = wu_ref[...].astype(wuo_ref.dtype)
    n_all = w_ref.shape[1]

    @pl.when(s == 0)
    def _():
        pj_ref[...] = jnp.zeros_like(pj_ref)
        t_idx = lax.broadcasted_iota(jnp.int32, (hw, hw), 0)
        s_idx = lax.broadcasted_iota(jnp.int32, (hw, hw), 1)
        lv = 32 - lax.clz(t_idx ^ s_idx)
        lvl_ref[...] = jnp.where(t_idx >= s_idx, lv, -1)
        lg = lbl_ref[...]
        e = jnp.exp(lg - jnp.max(lg, axis=0, keepdims=True))
        sm = e / jnp.sum(e, axis=0, keepdims=True)
        lb_ref[...] = jnp.sum(sm[: layer + 1], axis=0, keepdims=True)

    @pl.when(jnp.logical_or(s == 0, lax.rem(s - 1, chunks_per_seq) == 0))
    def _():
        hst_ref[...] = jnp.zeros_like(hst_ref)
        sst_ref[...] = jnp.zeros_like(sst_ref)
        xbuf_ref[:, 0:V7X_SUBLANES, :] = jnp.zeros((xbuf_ref.shape[0], V7X_SUBLANES, V7X_LANES), F32)

    hb = (_rms(x_ref[...]) * prew_ref[...]).astype(BF16)
    todo = [(lo, min(MIX_PIECE_COLS, n_all - lo)) for lo in range(0, n_all, MIX_PIECE_COLS)]
    new_proj = []

    def weight_cols(lo, width):
        blocks = []
        for j in range(lo // V7X_LANES, (lo + width) // V7X_LANES):
            src = j if j >= ssd_lo // V7X_LANES else (j % 4) * HG_HEADS + j // 4
            blocks.append(w_ref[:, src * V7X_LANES:(src + 1) * V7X_LANES])
        return blocks[0] if len(blocks) == 1 else jnp.concatenate(blocks, axis=1)

    def filler():
        if todo:
            lo, width = todo.pop(0)
            new_proj.append((lo, width, _dot(hb, weight_cols(lo, width))))

    _ssd_chunk(pj_ref[:, ssd_lo:], cw_ref, cb_ref, dtb_ref, alog_ref, dskip_ref, snw_ref, expand_ref,
               ob_ref, xbuf_ref, sst_ref, filler)
    for gi in range(HG_HEADS // grp):
        heads = []
        for g in range(grp):
            h = gi * grp + g
            base = h * HG_HEAD_COLS
            ks = slice(h * HG_DK, (h + 1) * HG_DK)
            vs = slice(h * HG_DV, (h + 1) * HG_DV)
            heads.append((pj_ref[:, base:base + HG_DK], pj_ref[:, base + HG_DK:base + 2 * HG_DK],
                          pj_ref[:, base + 2 * HG_DK:base + 2 * HG_DK + HG_DV],
                          pj_ref[:, base + 2 * HG_DK + HG_DV:base + HG_HEAD_COLS],
                          lb_ref[:, ks], hnw_ref[:, vs], hst_ref[h]))
        for g, (out, st_new) in enumerate(_hgrn_group(heads, lvl_ref, filler)):
            h = gi * grp + g
            hst_ref[h] = st_new
            oa_ref[:, h * HG_DV:(h + 1) * HG_DV] = out
    while todo:
        filler()

    for lo, width, val in new_proj:
        pj_ref[:, lo:lo + width] = val


def _mixer(x2, pre_w, w_all, lb_logits, hgrn_nw, conv_w, conv_b, dt_bias_p, a_log_p, d_skip_x,
           ssd_nw, expand, w_gate, w_up, seqlen, layer):
    t, d = x2.shape
    wc = MIX_CHUNK
    n_chunks = t // wc
    f = w_gate.shape[1]
    wf_rows = d // n_chunks
    assert wf_rows * n_chunks == d and wf_rows % (2 * V7X_SUBLANES) == 0
    n_all = w_all.shape[1]
    assert n_all == HG_HEADS * HG_HEAD_COLS + SSD_COLS
    xbuf_shape = ((SSD_WIDTH + SSD_BC_WIDTH) // V7X_LANES, V7X_SUBLANES + wc, V7X_LANES)
    blocks = (_nbytes((wc, d), F32) + _nbytes((wc, HG_VAL), MIX_DTYPE) + _nbytes((wc, SSD_WIDTH), MIX_DTYPE)
              + 2 * _nbytes((wf_rows, f), F32) + 2 * _nbytes((wf_rows, f), BF16))
    resident = _nbytes(w_all.shape, BF16) + 2 * _nbytes(expand.shape, BF16)
    scratch = (_nbytes((HG_HEADS, HG_DV, HG_DK), F32) + _nbytes((HGRN_HALF, HGRN_HALF), jnp.int32)
               + _nbytes((V7X_SUBLANES, HG_KEY), F32) + _nbytes(xbuf_shape, F32)
               + _nbytes((SSD_GROUPS, SSD_STATE, SSD_GROUP_WIDTH), F32) + _nbytes((wc, n_all), F32))
    values = _nbytes((wc, d), BF16) + _nbytes((wc, n_all), F32) + 8 * _nbytes((wc, SSD_WIDTH), F32)

    def whole(arr, **kw):
        return pl.BlockSpec(arr.shape, lambda s: (0,) * arr.ndim, **kw)

    def finished(width):
        return pl.BlockSpec((wc, width), lambda s: (jnp.maximum(s - 1, 0), 0))

    wf_spec = pl.BlockSpec((wf_rows, f), lambda s: (jnp.minimum(s, n_chunks - 1), 0))

    return pl.pallas_call(
        functools.partial(_mixer_body, layer=layer, chunks_per_seq=seqlen // wc),
        grid=(n_chunks + 1,),
        in_specs=[
            pl.BlockSpec((wc, d), lambda s: (jnp.minimum(s, n_chunks - 1), 0)),
            whole(pre_w),
            whole(w_all, pipeline_mode=pl.Buffered(1)),
            whole(lb_logits), whole(hgrn_nw), whole(conv_w), whole(conv_b), whole(dt_bias_p),
            whole(a_log_p), whole(d_skip_x), whole(ssd_nw), whole(expand), wf_spec, wf_spec,
        ],
        out_specs=[finished(HG_VAL), finished(SSD_WIDTH), wf_spec, wf_spec],
        out_shape=[jax.ShapeDtypeStruct((t, HG_VAL), MIX_DTYPE),
                   jax.ShapeDtypeStruct((t, SSD_WIDTH), MIX_DTYPE),
                   jax.ShapeDtypeStruct(w_gate.shape, BF16), jax.ShapeDtypeStruct(w_up.shape, BF16)],
        scratch_shapes=[
            pltpu.VMEM((HG_HEADS, HG_DV, HG_DK), F32),
            pltpu.VMEM((HGRN_HALF, HGRN_HALF), jnp.int32),
            pltpu.VMEM((1, HG_KEY), F32),
            pltpu.VMEM(xbuf_shape, F32),
            pltpu.VMEM((SSD_GROUPS, SSD_STATE, SSD_GROUP_WIDTH), F32),
            pltpu.VMEM((wc, n_all), F32),
        ],
        compiler_params=pltpu.CompilerParams(
            dimension_semantics=("arbitrary",),
            vmem_limit_bytes=_vmem_limit(blocks, resident + scratch, values),
        ),
        name="mixer",
    )(x2, pre_w, w_all, lb_logits, hgrn_nw, conv_w, conv_b, dt_bias_p, a_log_p, d_skip_x, ssd_nw, expand,
      w_gate, w_up)


def _outproj_body(oa_ref, ob_ref, wa_ref, wb_ref, x_ref, postw_ref, prew_ref, wd_ref,
                  x1_ref, h2_ref, wdo_ref):
    wdo_ref[...] = wd_ref[...].astype(wdo_ref.dtype)
    for r in range(x_ref.shape[0] // OUTPROJ_SUB):
        rs = slice(r * OUTPROJ_SUB, (r + 1) * OUTPROJ_SUB)
        mix = _dot(oa_ref[rs, :], wa_ref[...]) + _dot(ob_ref[rs, :], wb_ref[...])
        x1 = x_ref[rs, :] + _rms(mix) * postw_ref[...]
        x1_ref[rs, :] = x1
        h2_ref[rs, :] = (_rms(x1) * prew_ref[...]).astype(h2_ref.dtype)


def _out_proj(o_a, o_b, w_a, w_b, x2, post_w, pre_w, w_down):
    t, d = x2.shape
    tm = OUTPROJ_TM
    wd_rows = w_down.shape[0] // (t // tm)
    assert wd_rows * (t // tm) == w_down.shape[0] and wd_rows % (2 * V7X_SUBLANES) == 0
    blocks = (_nbytes((tm, HG_VAL), MIX_DTYPE) + _nbytes((tm, SSD_WIDTH), MIX_DTYPE)
              + _nbytes(w_a.shape, BF16) + _nbytes(w_b.shape, BF16) + 2 * _nbytes((tm, d), F32)
              + _nbytes((tm, d), BF16) + _nbytes((wd_rows, d), F32) + _nbytes((wd_rows, d), BF16))

    def row(i):
        return (i, 0)

    def fixed(i):
        return (0, 0)

    return pl.pallas_call(
        _outproj_body,
        grid=(t // tm,),
        in_specs=[
            pl.BlockSpec((tm, HG_VAL), row),
            pl.BlockSpec((tm, SSD_WIDTH), row),
            pl.BlockSpec(w_a.shape, fixed),
            pl.BlockSpec(w_b.shape, fixed),
            pl.BlockSpec((tm, d), row),
            pl.BlockSpec((1, d), fixed),
            pl.BlockSpec((1, d), fixed),
            pl.BlockSpec((wd_rows, d), row),
        ],
        out_specs=[pl.BlockSpec((tm, d), row), pl.BlockSpec((tm, d), row),
                   pl.BlockSpec((wd_rows, d), row)],
        out_shape=[jax.ShapeDtypeStruct((t, d), F32), jax.ShapeDtypeStruct((t, d), BF16),
                   jax.ShapeDtypeStruct(w_down.shape, BF16)],
        compiler_params=pltpu.CompilerParams(
            dimension_semantics=("parallel",),
            vmem_limit_bytes=_vmem_limit(blocks),
        ),
        name="out_proj",
    )(o_a, o_b, w_a, w_b, x2, post_w, pre_w, w_down)


def _ffn_body(h_ref, x1c_ref, wg_ref, wu_ref, wd_ref, postw_ref, o_ref, x1_ref):
    j = pl.program_id(1)
    d = o_ref.shape[1]
    xc = x1c_ref.shape[1]

    @pl.when(j == 0)
    def _():
        o_ref[...] = jnp.zeros_like(o_ref)

    @pl.when(j < d // xc)
    def _():
        x1_ref[:, pl.ds(pl.multiple_of(j * xc, xc), xc)] = x1c_ref[...]

    h = h_ref[...]
    hids = []
    for s in range(FFN_TF // FFN_SUB):
        cs = slice(s * FFN_SUB, (s + 1) * FFN_SUB)
        g = _dot(h, wg_ref[:, cs])
        u = _dot(h, wu_ref[:, cs])
        hids.append((_silu(g) * u).astype(BF16))
    hid = jnp.concatenate(hids, axis=1)
    for c in range(d // FFN_DOWN_COLS):
        cs = slice(c * FFN_DOWN_COLS, (c + 1) * FFN_DOWN_COLS)
        o_ref[:, cs] += _dot(hid, wd_ref[:, cs])

    @pl.when(j == pl.num_programs(1) - 1)
    def _():
        o_ref[...] = x1_ref[...] + _rms(o_ref[...]) * postw_ref[...]


def _ffn(h2, x1, w_gate, w_up, w_down, post_w):
    t, d = x1.shape
    f = w_gate.shape[1]
    tm, tf, xc = FFN_TM, FFN_TF, FFN_X1_COLS
    n_xc = d // xc
    assert f // tf >= n_xc
    blocks = (_nbytes((tm, d), BF16) + _nbytes((tm, xc), F32) + 2 * _nbytes((d, tf), BF16)
              + _nbytes((tf, d), BF16) + _nbytes((tm, d), F32))
    values = (2 * _nbytes((tm, tf), F32) + _nbytes((tm, tf), BF16) + _nbytes((tm, d), F32))
    return pl.pallas_call(
        _ffn_body,
        grid=(t // tm, f // tf),
        in_specs=[
            pl.BlockSpec((tm, d), lambda i, j: (i, 0)),
            pl.BlockSpec((tm, xc), lambda i, j: (i, jnp.minimum(j, n_xc - 1))),
            pl.BlockSpec((d, tf), lambda i, j: (0, j)),
            pl.BlockSpec((d, tf), lambda i, j: (0, j)),
            pl.BlockSpec((tf, d), lambda i, j: (j, 0)),
            pl.BlockSpec((1, d), lambda i, j: (0, 0)),
        ],
        out_specs=pl.BlockSpec((tm, d), lambda i, j: (i, 0)),
        out_shape=jax.ShapeDtypeStruct((t, d), F32),
        scratch_shapes=[pltpu.VMEM((tm, d), F32)],
        compiler_params=pltpu.CompilerParams(
            dimension_semantics=("parallel", "arbitrary"),
            vmem_limit_bytes=_vmem_limit(blocks, _nbytes((tm, d), F32), values),
        ),
        name="ffn",
    )(h2, x1, w_gate, w_up, w_down, post_w)


def _pad_lanes(v):
    return jnp.pad(v.astype(F32), (0, V7X_LANES - v.shape[0]))[None, :]


def _cast_w_in(w):
    n_all = HG_HEADS * HG_HEAD_COLS + SSD_COLS
    return jnp.pad(w, ((0, 0), (0, n_all - w.shape[1]))).astype(BF16)


def kernel(x, pre_mix_norm_w, w_in, lb_logits, conv_w, conv_b, dt_bias, a_log, d_skip, hgrn_norm_w,
           ssd_norm_w, w_out, post_mix_norm_w, pre_ffn_norm_w, w_gate, w_up, w_down, post_ffn_norm_w):
    batch, seqlen, d = x.shape
    depth = w_in.shape[0]
    expand = (jnp.arange(V7X_LANES)[:, None] == (jnp.arange(SSD_WIDTH)[None, :] // SSD_HEADDIM)).astype(BF16)

    x2 = x.reshape(batch * seqlen, d)
    for l in range(depth):
        o_a, o_b, wg_b, wu_b = _mixer(
            x2, pre_mix_norm_w[l][None, :], _cast_w_in(w_in[l]), lb_logits.astype(F32),
            hgrn_norm_w[l][None, :], conv_w[l], conv_b[l][None, :], _pad_lanes(dt_bias[l]),
            _pad_lanes(a_log[l]), jnp.repeat(d_skip[l].astype(F32), SSD_HEADDIM)[None, :],
            ssd_norm_w[l][None, :], expand, w_gate[l], w_up[l], seqlen, l)
        x1, h2, wd_b = _out_proj(o_a, o_b, w_out[l, :HG_VAL].astype(BF16), w_out[l, HG_VAL:].astype(BF16),
                                 x2, post_mix_norm_w[l][None, :], pre_ffn_norm_w[l][None, :], w_down[l])
        x2 = _ffn(h2, x1, wg_b, wu_b, wd_b, post_ffn_norm_w[l][None, :])
    return x2.reshape(batch, seqlen, d)
```

```python
import functools

import jax
import jax.numpy as jnp
from jax import lax
from jax.experimental import pallas as pl
from jax.experimental.pallas import tpu as pltpu

F32 = jnp.float32
BF16 = jnp.bfloat16

HG_HEADS = 8
HG_DK = 128
HG_DV = 128
HG_KEY = HG_HEADS * HG_DK
HG_VAL = HG_HEADS * HG_DV
SSD_HEADS = 16
SSD_HEADDIM = 64
SSD_WIDTH = SSD_HEADS * SSD_HEADDIM
SSD_GROUPS = 2
SSD_HPG = SSD_HEADS // SSD_GROUPS
SSD_STATE = 128
SSD_CONV = 4
SSD_GROUP_WIDTH = SSD_HPG * SSD_HEADDIM
SSD_BC_WIDTH = 2 * SSD_GROUPS * SSD_STATE
NORM_EPS = 1e-6

V7X_LANES = 128
V7X_SUBLANES = 8
V7X_VMEM_BYTES = 64 * 1024 * 1024
V7X_VMEM_COMPILER_RESERVE = 6 * 1024 * 1024
V7X_VMEM_UNSCOPED = 2 * 1024 * 1024

MIX_DTYPE = BF16
MIX_CHUNK = 256
HGRN_HALF = 128
HGRN_HEADS_PER_GROUP = 4
MIX_PIECE_COLS = 256
HG_HEAD_COLS = 2 * HG_DK + 2 * HG_DV
SSD_COLS = SSD_WIDTH + SSD_WIDTH + SSD_BC_WIDTH + V7X_LANES
OUTPROJ_TM = 512
OUTPROJ_SUB = 256
FFN_TM = 1024
FFN_TF = 512
FFN_SUB = 256
FFN_DOWN_COLS = 512
FFN_X1_COLS = 256
NEG_BIG = -1e30
LOG2_E = 1.4426950408889634


def _nbytes(shape, dtype):
    n = 1
    for s in shape:
        n *= s
    return n * jnp.dtype(dtype).itemsize


def _vmem_limit(block_bytes, scratch_bytes=0, value_bytes=0):
    need = 2 * block_bytes + scratch_bytes + value_bytes + V7X_VMEM_COMPILER_RESERVE
    return int(min(need, V7X_VMEM_BYTES - V7X_VMEM_UNSCOPED))


def _sigmoid(x):
    return 1.0 / (1.0 + jnp.exp2(x * (-LOG2_E)))


def _silu(x):
    return x * _sigmoid(x)


def _rms(x):
    return x * lax.rsqrt(jnp.mean(x * x, axis=-1, keepdims=True) + NORM_EPS)


def _dot(a, b):
    return jnp.dot(a, b, preferred_element_type=F32)


def _dot_nt(a, b):
    return lax.dot_general(a, b, (((1,), (1,)), ((), ())), preferred_element_type=F32)


def _split2(x, p):
    x4 = x.reshape(x.shape[0] // (2 * p), 2, p, x.shape[1])
    return x4[:, 0], x4[:, 1]


def _merge2(lo, hi):
    return jnp.stack([lo, hi], axis=1).reshape(-1, lo.shape[-1])


def _hgrn_group(heads, lvl_ref, filler):
    g = len(heads)
    w = MIX_CHUNK
    hw = HGRN_HALF
    sub = V7X_SUBLANES
    rows = g * w
    n_half = rows // hw
    assert w == 2 * hw and hw == V7X_LANES
    shp3 = (rows // sub, sub, HG_DK)
    sub_idx = lax.broadcasted_iota(jnp.int32, (1, sub, HG_DK), 1)

    q = jnp.concatenate([_silu(h[0]) * (HG_DK ** -0.5) for h in heads], axis=0)
    f = jnp.concatenate([h[4] + (1.0 - h[4]) * _sigmoid(h[1]) for h in heads], axis=0)
    k = 1.0 - f

    def half_scores(qn, kn, level, a_blocks):
        qb, kb = qn.astype(BF16), kn.astype(BF16)
        out = []
        for c in range(n_half):
            rs = slice(c * hw, (c + 1) * hw)
            s_c = _dot_nt(qb[rs], kb[rs])
            keep = lvl_ref[...] == level
            out.append(jnp.where(keep, s_c, 0.0 if a_blocks is None else a_blocks[c]))
        return out

    a_diag = half_scores(q, k, 0, None)
    filler()
    cq = f.reshape(shp3)
    ck = jnp.ones(shp3, F32)
    r = cq
    q3, k3 = q.reshape(shp3), k.reshape(shp3)
    level = 1
    p = 1
    while p < sub:
        a_diag = half_scores((q3 * cq).reshape(rows, HG_DK), (k3 * ck).reshape(rows, HG_DK), level, a_diag)
        upper = (sub_idx & p) != 0
        down = pltpu.roll(r, p, axis=1)
        up = pltpu.roll(r, sub - p, axis=1)
        cq = cq * jnp.where(upper, down, 1.0)
        ck = ck * jnp.where(upper, 1.0, up)
        r = r * jnp.where(upper, down, up)
        filler()
        p *= 2
        level += 1
    cq = cq.reshape(rows, HG_DK)
    ck = ck.reshape(rows, HG_DK)

    a_cross = None
    while p < w:
        cq_lo, cq_hi = _split2(cq, p)
        ck_lo, ck_hi = _split2(ck, p)
        q_lo, q_hi = _split2(q, p)
        k_lo, k_hi = _split2(k, p)
        nb = rows // (2 * p)
        if p < hw:
            qu = (q_hi * cq_hi).reshape(rows // 2, HG_DK).astype(BF16)
            kn = _merge2(k_lo * ck_lo, k_hi).astype(BF16)
            new_a = []
            for c in range(n_half):
                s_c = _dot_nt(qu[c * (hw // 2):(c + 1) * (hw // 2)], kn[c * hw:(c + 1) * hw])
                a_lo, a_hi = _split2(a_diag[c], p)
                _, lv_hi = _split2(lvl_ref[...], p)
                a_hi = jnp.where(lv_hi == level, s_c.reshape(a_hi.shape), a_hi)
                new_a.append(_merge2(a_lo, a_hi))
            a_diag = new_a
        else:
            qu = (q_hi * cq_hi).astype(BF16)
            kl = (k_lo * ck_lo).astype(BF16)
            a_cross = [_dot_nt(qu[i], kl[i]) for i in range(g)]
        r4 = r.reshape(nb, 2, sub, HG_DK)
        r_lo, r_hi = r4[:, 0], r4[:, 1]
        cq_hi = (cq_hi.reshape(nb, p // sub, sub, HG_DK) * r_lo[:, None]).reshape(nb, p, HG_DK)
        ck_lo = (ck_lo.reshape(nb, p // sub, sub, HG_DK) * r_hi[:, None]).reshape(nb, p, HG_DK)
        cq = _merge2(cq_lo, cq_hi)
        ck = _merge2(ck_lo, ck_hi)
        r = r_lo * r_hi
        filler()
        p *= 2
        level += 1

    qn = (q * cq).astype(BF16)
    kn = (k * ck).astype(BF16)
    results = []
    for i, (_, _, v, gate, _, nw, st) in enumerate(heads):
        rs = slice(i * w, (i + 1) * w)
        vb = v.astype(BF16)
        o_inter = _dot_nt(qn[rs], st.astype(BF16))
        o0 = _dot(a_diag[2 * i].astype(BF16), vb[:hw])
        o1 = _dot(jnp.concatenate([a_cross[i], a_diag[2 * i + 1]], axis=1).astype(BF16), vb)
        o = jnp.concatenate([o0, o1], axis=0) + o_inter
        st_new = st * r[i, 0:1, :] + _dot(v.T.astype(BF16), kn[rs])
        results.append(((_rms(o) * nw * _silu(gate)).astype(MIX_DTYPE), st_new))
    return results


def _split3(x):
    hi = x.astype(BF16)
    r1 = x - hi.astype(F32)
    mid = r1.astype(BF16)
    lo = (r1 - mid.astype(F32)).astype(BF16)
    return hi, mid, lo


def _dot_exact01_left(m01, x):
    hi, mid, lo = _split3(x)
    return _dot(m01, hi) + _dot(m01, mid) + _dot(m01, lo)


def _dot_expand(x, m01):
    hi = x.astype(BF16)
    mid = (x - hi.astype(F32)).astype(BF16)
    return _dot(hi, m01) + _dot(mid, m01)


def _ssd_chunk(ps, cw_ref, cb_ref, dtb_ref, alog_ref, dskip_ref, nw_ref, expand_ref, o_ref,
               xbuf_ref, state_ref, filler):
    wc = MIX_CHUNK
    hdr = V7X_SUBLANES
    gw = SSD_GROUP_WIDTH
    z = ps[:, :SSD_WIDTH]
    xbc = ps[:, SSD_WIDTH:SSD_WIDTH + SSD_WIDTH + SSD_BC_WIDTH]
    dt_raw = ps[:, SSD_WIDTH + SSD_WIDTH + SSD_BC_WIDTH:]

    n_slabs = xbc.shape[1] // V7X_LANES
    cols = []
    for si in range(n_slabs):
        cs = slice(si * V7X_LANES, (si + 1) * V7X_LANES)
        xbuf_ref[si, hdr:hdr + wc, :] = xbc[:, cs]
        acc = cb_ref[:, cs]
        for kk in range(SSD_CONV):
            off = hdr - (SSD_CONV - 1) + kk
            acc = acc + cw_ref[kk:kk + 1, cs] * xbuf_ref[si, off:off + wc, :]
        xbuf_ref[si, 0:hdr, :] = xbuf_ref[si, wc:wc + hdr, :]
        cols.append(_silu(acc))
    per_group = gw // V7X_LANES
    xs = [jnp.concatenate(cols[g * per_group:(g + 1) * per_group], axis=1) for g in range(SSD_GROUPS)]
    bcs = jnp.concatenate(cols[SSD_GROUPS * per_group:], axis=1)
    filler()

    dtr = dt_raw + dtb_ref[...]
    dt = jnp.maximum(dtr, 0.0) + jnp.log1p(jnp.exp(-jnp.abs(dtr)))
    a = dt * (-jnp.exp(alog_ref[...]))
    t_idx = lax.broadcasted_iota(jnp.int32, (wc, wc), 0)
    s_idx = lax.broadcasted_iota(jnp.int32, (wc, wc), 1)
    causal = t_idx >= s_idx
    tril01 = jnp.where(causal, 1.0, 0.0).astype(BF16)
    acs = _dot_exact01_left(tril01, a) * LOG2_E
    acs_t = acs.T
    filler()

    expand = expand_ref[...]
    acs_x = _dot_expand(acs, expand)
    dt_x = _dot_expand(dt, expand)
    filler()
    lane = lax.broadcasted_iota(jnp.int32, (wc, V7X_LANES), 1)
    first_half = lane < SSD_HEADDIM

    for g in range(SSD_GROUPS):
        gs = slice(g * gw, (g + 1) * gw)
        xs_g = xs[g]
        xdt = xs_g * dt_x[:, gs]
        xdt_b = xdt.astype(BF16)
        b_g = bcs[:, g * SSD_STATE:(g + 1) * SSD_STATE]
        c_g = bcs[:, (SSD_GROUPS + g) * SSD_STATE:(SSD_GROUPS + g + 1) * SSD_STATE]
        c_b = c_g.astype(BF16)
        cb = _dot_nt(c_b, b_g.astype(BF16))

        pieces = []
        for j in range(SSD_HPG // 2):
            filler()
            xp = xdt_b[:, j * V7X_LANES:(j + 1) * V7X_LANES]
            acc = None
            for half in range(2):
                h = g * SSD_HPG + 2 * j + half
                seg = acs[:, h:h + 1] - acs_t[h:h + 1, :]
                m = (cb * jnp.exp2(jnp.where(causal, seg, NEG_BIG))).astype(BF16)
                keep = first_half if half == 0 else jnp.logical_not(first_half)
                part = _dot(m, jnp.where(keep, xp, jnp.zeros_like(xp)))
                acc = part if acc is None else acc + part
            pieces.append(acc)
        y_diag = jnp.concatenate(pieces, axis=-1)

        st = state_ref[g]
        acs_g = acs_x[:, gs]
        y_off = _dot(c_b, st.astype(BF16)) * jnp.exp2(acs_g)
        y = y_diag + y_off + dskip_ref[:, gs] * xs_g

        last = acs_g[wc - 1:wc, :]
        xdec = (xdt * jnp.exp2(last - acs_g)).astype(BF16)
        state_ref[g] = st * jnp.exp2(last) + _dot(b_g.T.astype(BF16), xdec)

        yz = y * _silu(z[:, gs])
        o_ref[:, gs] = (_rms(yz) * nw_ref[:, gs]).astype(o_ref.dtype)


def _mixer_body(x_ref, prew_ref, w_ref, lbl_ref, hnw_ref, cw_ref, cb_ref, dtb_ref, alog_ref,
                dskip_ref, snw_ref, expand_ref, wg_ref, wu_ref, oa_ref, ob_ref, wgo_ref, wuo_ref,
                hst_ref, lvl_ref, lb_ref, xbuf_ref, sst_ref, pj_ref, wdt_ref, *, layer, chunks_per_seq):
    hw = HGRN_HALF
    s = pl.program_id(0)
    grp = HGRN_HEADS_PER_GROUP
    ssd_lo = HG_HEADS * HG_HEAD_COLS
    dt_lo = ssd_lo + SSD_COLS - V7X_LANES
    wgo_ref[...] = wg_ref[...].astype(wgo_ref.dtype)
    wuo_ref[...] = wu_ref[...].astype(wuo_ref.dtype)
    n_all = HG_HEADS * HG_HEAD_COLS + SSD_COLS
    n_in = w_ref.shape[1]

    @pl.when(s == 0)
    def _():
        pj_ref[...] = jnp.zeros_like(pj_ref)
        wdt_ref[...] = jnp.zeros_like(wdt_ref)
        wdt_ref[:, 0:n_in - dt_lo] = w_ref[:, dt_lo:n_in]
        t_idx = lax.broadcasted_iota(jnp.int32, (hw, hw), 0)
        s_idx = lax.broadcasted_iota(jnp.int32, (hw, hw), 1)
        lv = 32 - lax.clz(t_idx ^ s_idx)
        lvl_ref[...] = jnp.where(t_idx >= s_idx, lv, -1)
        lg = lbl_ref[...]
        e = jnp.exp(lg - jnp.max(lg, axis=0, keepdims=True))
        sm = e / jnp.sum(e, axis=0, keepdims=True)
        lb_ref[...] = jnp.sum(sm[: layer + 1], axis=0, keepdims=True)

    @pl.when(jnp.logical_or(s == 0, lax.rem(s - 1, chunks_per_seq) == 0))
    def _():
        hst_ref[...] = jnp.zeros_like(hst_ref)
        sst_ref[...] = jnp.zeros_like(sst_ref)
        xbuf_ref[:, 0:V7X_SUBLANES, :] = jnp.zeros((xbuf_ref.shape[0], V7X_SUBLANES, V7X_LANES), F32)

    hb = (_rms(x_ref[...]) * prew_ref[...]).astype(BF16)
    todo = [(lo, min(MIX_PIECE_COLS, n_all - lo)) for lo in range(0, n_all, MIX_PIECE_COLS)]
    new_proj = []

    def weight_cols(lo, width):
        blocks = []
        for j in range(lo // V7X_LANES, (lo + width) // V7X_LANES):
            src = j if j >= ssd_lo // V7X_LANES else (j % 4) * HG_HEADS + j // 4
            if src * V7X_LANES == dt_lo:
                blocks.append(wdt_ref[...])
            else:
                blocks.append(w_ref[:, src * V7X_LANES:(src + 1) * V7X_LANES])
        return blocks[0] if len(blocks) == 1 else jnp.concatenate(blocks, axis=1)

    def filler():
        if todo:
            lo, width = todo.pop(0)
            new_proj.append((lo, width, _dot(hb, weight_cols(lo, width))))

    _ssd_chunk(pj_ref[:, ssd_lo:], cw_ref, cb_ref, dtb_ref, alog_ref, dskip_ref, snw_ref, expand_ref,
               ob_ref, xbuf_ref, sst_ref, filler)
    for gi in range(HG_HEADS // grp):
        heads = []
        for g in range(grp):
            h = gi * grp + g
            base = h * HG_HEAD_COLS
            ks = slice(h * HG_DK, (h + 1) * HG_DK)
            vs = slice(h * HG_DV, (h + 1) * HG_DV)
            heads.append((pj_ref[:, base:base + HG_DK], pj_ref[:, base + HG_DK:base + 2 * HG_DK],
                          pj_ref[:, base + 2 * HG_DK:base + 2 * HG_DK + HG_DV],
                          pj_ref[:, base + 2 * HG_DK + HG_DV:base + HG_HEAD_COLS],
                          lb_ref[:, ks], hnw_ref[:, vs], hst_ref[h]))
        for g, (out, st_new) in enumerate(_hgrn_group(heads, lvl_ref, filler)):
            h = gi * grp + g
            hst_ref[h] = st_new
            oa_ref[:, h * HG_DV:(h + 1) * HG_DV] = out
    while todo:
        filler()

    for lo, width, val in new_proj:
        pj_ref[:, lo:lo + width] = val


def _mixer(x2, pre_w, w_all, lb_logits, hgrn_nw, conv_w, conv_b, dt_bias_p, a_log_p, d_skip_x,
           ssd_nw, expand, w_gate, w_up, seqlen, layer):
    t, d = x2.shape
    wc = MIX_CHUNK
    n_chunks = t // wc
    f = w_gate.shape[1]
    wf_rows = d // n_chunks
    assert wf_rows * n_chunks == d and wf_rows % (2 * V7X_SUBLANES) == 0
    n_all = HG_HEADS * HG_HEAD_COLS + SSD_COLS
    assert w_all.shape[1] == n_all - V7X_LANES + SSD_HEADS
    xbuf_shape = ((SSD_WIDTH + SSD_BC_WIDTH) // V7X_LANES, V7X_SUBLANES + wc, V7X_LANES)
    blocks = (_nbytes((wc, d), F32) + _nbytes((wc, HG_VAL), MIX_DTYPE) + _nbytes((wc, SSD_WIDTH), MIX_DTYPE)
              + 2 * _nbytes((wf_rows, f), F32) + 2 * _nbytes((wf_rows, f), BF16))
    resident = _nbytes(w_all.shape, BF16) + 2 * _nbytes(expand.shape, BF16)
    scratch = (_nbytes((HG_HEADS, HG_DV, HG_DK), F32) + _nbytes((HGRN_HALF, HGRN_HALF), jnp.int32)
               + _nbytes((V7X_SUBLANES, HG_KEY), F32) + _nbytes(xbuf_shape, F32)
               + _nbytes((SSD_GROUPS, SSD_STATE, SSD_GROUP_WIDTH), F32) + _nbytes((wc, n_all), F32))
    values = _nbytes((wc, d), BF16) + _nbytes((wc, n_all), F32) + 8 * _nbytes((wc, SSD_WIDTH), F32)

    def whole(arr, **kw):
        return pl.BlockSpec(arr.shape, lambda s: (0,) * arr.ndim, **kw)

    def finished(width):
        return pl.BlockSpec((wc, width), lambda s: (jnp.maximum(s - 1, 0), 0))

    wf_spec = pl.BlockSpec((wf_rows, f), lambda s: (jnp.minimum(s, n_chunks - 1), 0))

    return pl.pallas_call(
        functools.partial(_mixer_body, layer=layer, chunks_per_seq=seqlen // wc),
        grid=(n_chunks + 1,),
        in_specs=[
            pl.BlockSpec((wc, d), lambda s: (jnp.minimum(s, n_chunks - 1), 0)),
            whole(pre_w),
            whole(w_all, pipeline_mode=pl.Buffered(1)),
            whole(lb_logits), whole(hgrn_nw), whole(conv_w), whole(conv_b), whole(dt_bias_p),
            whole(a_log_p), whole(d_skip_x), whole(ssd_nw), whole(expand), wf_spec, wf_spec,
        ],
        out_specs=[finished(HG_VAL), finished(SSD_WIDTH), wf_spec, wf_spec],
        out_shape=[jax.ShapeDtypeStruct((t, HG_VAL), MIX_DTYPE),
                   jax.ShapeDtypeStruct((t, SSD_WIDTH), MIX_DTYPE),
                   jax.ShapeDtypeStruct(w_gate.shape, BF16), jax.ShapeDtypeStruct(w_up.shape, BF16)],
        scratch_shapes=[
            pltpu.VMEM((HG_HEADS, HG_DV, HG_DK), F32),
            pltpu.VMEM((HGRN_HALF, HGRN_HALF), jnp.int32),
            pltpu.VMEM((1, HG_KEY), F32),
            pltpu.VMEM(xbuf_shape, F32),
            pltpu.VMEM((SSD_GROUPS, SSD_STATE, SSD_GROUP_WIDTH), F32),
            pltpu.VMEM((wc, n_all), F32),
            pltpu.VMEM((d, V7X_LANES), BF16),
        ],
        compiler_params=pltpu.CompilerParams(
            dimension_semantics=("arbitrary",),
            vmem_limit_bytes=_vmem_limit(blocks, resident + scratch, values),
        ),
        name="mixer",
    )(x2, pre_w, w_all, lb_logits, hgrn_nw, conv_w, conv_b, dt_bias_p, a_log_p, d_skip_x, ssd_nw, expand,
      w_gate, w_up)


def _outproj_body(oa_ref, ob_ref, wa_ref, wb_ref, x_ref, postw_ref, prew_ref, wd_ref,
                  x1_ref, h2_ref, wdo_ref):
    wdo_ref[...] = wd_ref[...].astype(wdo_ref.dtype)
    for r in range(x_ref.shape[0] // OUTPROJ_SUB):
        rs = slice(r * OUTPROJ_SUB, (r + 1) * OUTPROJ_SUB)
        mix = _dot(oa_ref[rs, :], wa_ref[...]) + _dot(ob_ref[rs, :], wb_ref[...])
        x1 = x_ref[rs, :] + _rms(mix) * postw_ref[...]
        x1_ref[rs, :] = x1
        h2_ref[rs, :] = (_rms(x1) * prew_ref[...]).astype(h2_ref.dtype)


def _out_proj(o_a, o_b, w_a, w_b, x2, post_w, pre_w, w_down):
    t, d = x2.shape
    tm = OUTPROJ_TM
    wd_rows = w_down.shape[0] // (t // tm)
    assert wd_rows * (t // tm) == w_down.shape[0] and wd_rows % (2 * V7X_SUBLANES) == 0
    blocks = (_nbytes((tm, HG_VAL), MIX_DTYPE) + _nbytes((tm, SSD_WIDTH), MIX_DTYPE)
              + _nbytes(w_a.shape, BF16) + _nbytes(w_b.shape, BF16) + 2 * _nbytes((tm, d), F32)
              + _nbytes((tm, d), BF16) + _nbytes((wd_rows, d), F32) + _nbytes((wd_rows, d), BF16))

    def row(i):
        return (i, 0)

    def fixed(i):
        return (0, 0)

    return pl.pallas_call(
        _outproj_body,
        grid=(t // tm,),
        in_specs=[
            pl.BlockSpec((tm, HG_VAL), row),
            pl.BlockSpec((tm, SSD_WIDTH), row),
            pl.BlockSpec(w_a.shape, fixed),
            pl.BlockSpec(w_b.shape, fixed),
            pl.BlockSpec((tm, d), row),
            pl.BlockSpec((1, d), fixed),
            pl.BlockSpec((1, d), fixed),
            pl.BlockSpec((wd_rows, d), row),
        ],
        out_specs=[pl.BlockSpec((tm, d), row), pl.BlockSpec((tm, d), row),
                   pl.BlockSpec((wd_rows, d), row)],
        out_shape=[jax.ShapeDtypeStruct((t, d), F32), jax.ShapeDtypeStruct((t, d), BF16),
                   jax.ShapeDtypeStruct(w_down.shape, BF16)],
        compiler_params=pltpu.CompilerParams(
            dimension_semantics=("parallel",),
            vmem_limit_bytes=_vmem_limit(blocks),
        ),
        name="out_proj",
    )(o_a, o_b, w_a, w_b, x2, post_w, pre_w, w_down)


def _ffn_body(h_ref, x1c_ref, wg_ref, wu_ref, wd_ref, postw_ref, o_ref, x1_ref):
    j = pl.program_id(1)
    d = o_ref.shape[1]
    xc = x1c_ref.shape[1]

    @pl.when(j == 0)
    def _():
        o_ref[...] = jnp.zeros_like(o_ref)

    @pl.when(j < d // xc)
    def _():
        x1_ref[:, pl.ds(pl.multiple_of(j * xc, xc), xc)] = x1c_ref[...]

    h = h_ref[...]
    hids = []
    for s in range(FFN_TF // FFN_SUB):
        cs = slice(s * FFN_SUB, (s + 1) * FFN_SUB)
        g = _dot(h, wg_ref[:, cs])
        u = _dot(h, wu_ref[:, cs])
        hids.append((_silu(g) * u).astype(BF16))
    hid = jnp.concatenate(hids, axis=1)
    for c in range(d // FFN_DOWN_COLS):
        cs = slice(c * FFN_DOWN_COLS, (c + 1) * FFN_DOWN_COLS)
        o_ref[:, cs] += _dot(hid, wd_ref[:, cs])

    @pl.when(j == pl.num_programs(1) - 1)
    def _():
        o_ref[...] = x1_ref[...] + _rms(o_ref[...]) * postw_ref[...]


def _ffn(h2, x1, w_gate, w_up, w_down, post_w):
    t, d = x1.shape
    f = w_gate.shape[1]
    tm, tf, xc = FFN_TM, FFN_TF, FFN_X1_COLS
    n_xc = d // xc
    assert f // tf >= n_xc
    blocks = (_nbytes((tm, d), BF16) + _nbytes((tm, xc), F32) + 2 * _nbytes((d, tf), BF16)
              + _nbytes((tf, d), BF16) + _nbytes((tm, d), F32))
    values = (2 * _nbytes((tm, tf), F32) + _nbytes((tm, tf), BF16) + _nbytes((tm, d), F32))
    return pl.pallas_call(
        _ffn_body,
        grid=(t // tm, f // tf),
        in_specs=[
            pl.BlockSpec((tm, d), lambda i, j: (i, 0)),
            pl.BlockSpec((tm, xc), lambda i, j: (i, jnp.minimum(j, n_xc - 1))),
            pl.BlockSpec((d, tf), lambda i, j: (0, j)),
            pl.BlockSpec((d, tf), lambda i, j: (0, j)),
            pl.BlockSpec((tf, d), lambda i, j: (j, 0)),
            pl.BlockSpec((1, d), lambda i, j: (0, 0)),
        ],
        out_specs=pl.BlockSpec((tm, d), lambda i, j: (i, 0)),
        out_shape=jax.ShapeDtypeStruct((t, d), F32),
        scratch_shapes=[pltpu.VMEM((tm, d), F32)],
        compiler_params=pltpu.CompilerParams(
            dimension_semantics=("parallel", "arbitrary"),
            vmem_limit_bytes=_vmem_limit(blocks, _nbytes((tm, d), F32), values),
        ),
        name="ffn",
    )(h2, x1, w_gate, w_up, w_down, post_w)


def _pad_lanes(v):
    return jnp.pad(v.astype(F32), (0, V7X_LANES - v.shape[0]))[None, :]


def kernel(x, pre_mix_norm_w, w_in, lb_logits, conv_w, conv_b, dt_bias, a_log, d_skip, hgrn_norm_w,
           ssd_norm_w, w_out, post_mix_norm_w, pre_ffn_norm_w, w_gate, w_up, w_down, post_ffn_norm_w):
    batch, seqlen, d = x.shape
    depth = w_in.shape[0]
    expand = (jnp.arange(V7X_LANES)[:, None] == (jnp.arange(SSD_WIDTH)[None, :] // SSD_HEADDIM)).astype(BF16)

    x2 = x.reshape(batch * seqlen, d)
    for l in range(depth):
        o_a, o_b, wg_b, wu_b = _mixer(
            x2, pre_mix_norm_w[l][None, :], w_in[l].astype(BF16), lb_logits.astype(F32),
            hgrn_norm_w[l][None, :], conv_w[l], conv_b[l][None, :], _pad_lanes(dt_bias[l]),
            _pad_lanes(a_log[l]), jnp.repeat(d_skip[l].astype(F32), SSD_HEADDIM)[None, :],
            ssd_norm_w[l][None, :], expand, w_gate[l], w_up[l], seqlen, l)
        x1, h2, wd_b = _out_proj(o_a, o_b, w_out[l, :HG_VAL].astype(BF16), w_out[l, HG_VAL:].astype(BF16),
                                 x2, post_mix_norm_w[l][None, :], pre_ffn_norm_w[l][None, :], w_down[l])
        x2 = _ffn(h2, x1, wg_b, wu_b, wd_b, post_ffn_norm_w[l][None, :])
    return x2.reshape(batch, seqlen, d)
```

```python
import functools

import jax
import jax.numpy as jnp
from jax import lax
from jax.experimental import pallas as pl
from jax.experimental.pallas import tpu as pltpu

F32 = jnp.float32
BF16 = jnp.bfloat16

HG_HEADS = 8
HG_DK = 128
HG_DV = 128
HG_KEY = HG_HEADS * HG_DK
HG_VAL = HG_HEADS * HG_DV
SSD_HEADS = 16
SSD_HEADDIM = 64
SSD_WIDTH = SSD_HEADS * SSD_HEADDIM
SSD_GROUPS = 2
SSD_HPG = SSD_HEADS // SSD_GROUPS
SSD_STATE = 128
SSD_CONV = 4
SSD_GROUP_WIDTH = SSD_HPG * SSD_HEADDIM
SSD_BC_WIDTH = 2 * SSD_GROUPS * SSD_STATE
NORM_EPS = 1e-6

V7X_LANES = 128
V7X_SUBLANES = 8
V7X_VMEM_BYTES = 64 * 1024 * 1024
V7X_VMEM_COMPILER_RESERVE = 6 * 1024 * 1024
V7X_VMEM_UNSCOPED = 2 * 1024 * 1024

MIX_DTYPE = BF16
MIX_CHUNK = 256
HGRN_HALF = 128
HGRN_HEADS_PER_GROUP = 4
MIX_PIECE_COLS = 256
HG_HEAD_COLS = 2 * HG_DK + 2 * HG_DV
SSD_COLS = SSD_WIDTH + SSD_WIDTH + SSD_BC_WIDTH + V7X_LANES
OUTPROJ_TM = 512
OUTPROJ_SUB = 256
FFN_TM = 1024
FFN_TF = 512
FFN_SUB = 256
FFN_DOWN_COLS = 512
FFN_X1_COLS = 256
NEG_BIG = -1e30
LOG2_E = 1.4426950408889634


def _nbytes(shape, dtype):
    n = 1
    for s in shape:
        n *= s
    return n * jnp.dtype(dtype).itemsize


def _vmem_limit(block_bytes, scratch_bytes=0, value_bytes=0):
    need = 2 * block_bytes + scratch_bytes + value_bytes + V7X_VMEM_COMPILER_RESERVE
    return int(min(need, V7X_VMEM_BYTES - V7X_VMEM_UNSCOPED))


def _sigmoid(x):
    return 1.0 / (1.0 + jnp.exp2(x * (-LOG2_E)))


def _silu(x):
    return x * _sigmoid(x)


def _rms(x):
    return x * lax.rsqrt(jnp.mean(x * x, axis=-1, keepdims=True) + NORM_EPS)


def _dot(a, b):
    return jnp.dot(a, b, preferred_element_type=F32)


def _dot_nt(a, b):
    return lax.dot_general(a, b, (((1,), (1,)), ((), ())), preferred_element_type=F32)


def _split2(x, p):
    x4 = x.reshape(x.shape[0] // (2 * p), 2, p, x.shape[1])
    return x4[:, 0], x4[:, 1]


def _merge2(lo, hi):
    return jnp.stack([lo, hi], axis=1).reshape(-1, lo.shape[-1])


def _hgrn_group(heads, lvl_ref, filler):
    g = len(heads)
    w = MIX_CHUNK
    hw = HGRN_HALF
    sub = V7X_SUBLANES
    rows = g * w
    n_half = rows // hw
    assert w == 2 * hw and hw == V7X_LANES
    shp3 = (rows // sub, sub, HG_DK)
    sub_idx = lax.broadcasted_iota(jnp.int32, (1, sub, HG_DK), 1)

    q = jnp.concatenate([_silu(h[0]) * (HG_DK ** -0.5) for h in heads], axis=0)
    f = jnp.concatenate([h[4] + (1.0 - h[4]) * _sigmoid(h[1]) for h in heads], axis=0)
    k = 1.0 - f

    def half_scores(qn, kn, level, a_blocks):
        qb, kb = qn.astype(BF16), kn.astype(BF16)
        out = []
        for c in range(n_half):
            rs = slice(c * hw, (c + 1) * hw)
            s_c = _dot_nt(qb[rs], kb[rs])
            keep = lvl_ref[...] == level
            out.append(jnp.where(keep, s_c, 0.0 if a_blocks is None else a_blocks[c]))
        return out

    a_diag = half_scores(q, k, 0, None)
    filler()
    cq = f.reshape(shp3)
    ck = jnp.ones(shp3, F32)
    r = cq
    q3, k3 = q.reshape(shp3), k.reshape(shp3)
    level = 1
    p = 1
    while p < sub:
        a_diag = half_scores((q3 * cq).reshape(rows, HG_DK), (k3 * ck).reshape(rows, HG_DK), level, a_diag)
        upper = (sub_idx & p) != 0
        down = pltpu.roll(r, p, axis=1)
        up = pltpu.roll(r, sub - p, axis=1)
        cq = cq * jnp.where(upper, down, 1.0)
        ck = ck * jnp.where(upper, 1.0, up)
        r = r * jnp.where(upper, down, up)
        filler()
        p *= 2
        level += 1
    cq = cq.reshape(rows, HG_DK)
    ck = ck.reshape(rows, HG_DK)

    a_cross = None
    while p < w:
        cq_lo, cq_hi = _split2(cq, p)
        ck_lo, ck_hi = _split2(ck, p)
        q_lo, q_hi = _split2(q, p)
        k_lo, k_hi = _split2(k, p)
        nb = rows // (2 * p)
        if p < hw:
            qu = (q_hi * cq_hi).reshape(rows // 2, HG_DK).astype(BF16)
            kn = _merge2(k_lo * ck_lo, k_hi).astype(BF16)
            new_a = []
            for c in range(n_half):
                s_c = _dot_nt(qu[c * (hw // 2):(c + 1) * (hw // 2)], kn[c * hw:(c + 1) * hw])
                a_lo, a_hi = _split2(a_diag[c], p)
                _, lv_hi = _split2(lvl_ref[...], p)
                a_hi = jnp.where(lv_hi == level, s_c.reshape(a_hi.shape), a_hi)
                new_a.append(_merge2(a_lo, a_hi))
            a_diag = new_a
        else:
            qu = (q_hi * cq_hi).astype(BF16)
            kl = (k_lo * ck_lo).astype(BF16)
            a_cross = [_dot_nt(qu[i], kl[i]) for i in range(g)]
        r4 = r.reshape(nb, 2, sub, HG_DK)
        r_lo, r_hi = r4[:, 0], r4[:, 1]
        cq_hi = (cq_hi.reshape(nb, p // sub, sub, HG_DK) * r_lo[:, None]).reshape(nb, p, HG_DK)
        ck_lo = (ck_lo.reshape(nb, p // sub, sub, HG_DK) * r_hi[:, None]).reshape(nb, p, HG_DK)
        cq = _merge2(cq_lo, cq_hi)
        ck = _merge2(ck_lo, ck_hi)
        r = r_lo * r_hi
        filler()
        p *= 2
        level += 1

    qn = (q * cq).astype(BF16)
    kn = (k * ck).astype(BF16)
    results = []
    for i, (_, _, v, gate, _, nw, st) in enumerate(heads):
        rs = slice(i * w, (i + 1) * w)
        vb = v.astype(BF16)
        o_inter = _dot_nt(qn[rs], st.astype(BF16))
        o0 = _dot(a_diag[2 * i].astype(BF16), vb[:hw])
        o1 = _dot(jnp.concatenate([a_cross[i], a_diag[2 * i + 1]], axis=1).astype(BF16), vb)
        o = jnp.concatenate([o0, o1], axis=0) + o_inter
        st_new = st * r[i, 0:1, :] + _dot(v.T.astype(BF16), kn[rs])
        results.append(((_rms(o) * nw * _silu(gate)).astype(MIX_DTYPE), st_new))
    return results


def _split3(x):
    hi = x.astype(BF16)
    r1 = x - hi.astype(F32)
    mid = r1.astype(BF16)
    lo = (r1 - mid.astype(F32)).astype(BF16)
    return hi, mid, lo


def _dot_exact01_left(m01, x):
    hi, mid, lo = _split3(x)
    return _dot(m01, hi) + _dot(m01, mid) + _dot(m01, lo)


def _ssd_chunk(ps, cw_ref, cb_ref, dtb_ref, alog_ref, dskip_ref, nw_ref, o_ref, xbuf_ref, state_ref,
               filler):
    wc = MIX_CHUNK
    hdr = V7X_SUBLANES
    gw = SSD_GROUP_WIDTH
    z = ps[:, :SSD_WIDTH]
    xbc = ps[:, SSD_WIDTH:SSD_WIDTH + SSD_WIDTH + SSD_BC_WIDTH]
    dt_raw = ps[:, SSD_WIDTH + SSD_WIDTH + SSD_BC_WIDTH:]

    n_slabs = xbc.shape[1] // V7X_LANES
    cols = []
    for si in range(n_slabs):
        cs = slice(si * V7X_LANES, (si + 1) * V7X_LANES)
        xbuf_ref[si, hdr:hdr + wc, :] = xbc[:, cs]
        acc = cb_ref[:, cs]
        for kk in range(SSD_CONV):
            off = hdr - (SSD_CONV - 1) + kk
            acc = acc + cw_ref[kk:kk + 1, cs] * xbuf_ref[si, off:off + wc, :]
        xbuf_ref[si, 0:hdr, :] = xbuf_ref[si, wc:wc + hdr, :]
        cols.append(_silu(acc))
    per_group = gw // V7X_LANES
    xs = [jnp.concatenate(cols[g * per_group:(g + 1) * per_group], axis=1) for g in range(SSD_GROUPS)]
    bcs = jnp.concatenate(cols[SSD_GROUPS * per_group:], axis=1)
    filler()

    dtr = dt_raw + dtb_ref[...]
    dt = jnp.maximum(dtr, 0.0) + jnp.log1p(jnp.exp(-jnp.abs(dtr)))
    a = dt * (-jnp.exp(alog_ref[...]))
    t_idx = lax.broadcasted_iota(jnp.int32, (wc, wc), 0)
    s_idx = lax.broadcasted_iota(jnp.int32, (wc, wc), 1)
    causal = t_idx >= s_idx
    tril01 = jnp.where(causal, 1.0, 0.0).astype(BF16)
    acs = _dot_exact01_left(tril01, a) * LOG2_E
    acs_t = acs.T
    filler()

    lane = lax.broadcasted_iota(jnp.int32, (wc, V7X_LANES), 1)
    first_half = lane < SSD_HEADDIM

    def expand_heads(v):
        tiles = [jnp.where(first_half, v[:, 2 * j:2 * j + 1], v[:, 2 * j + 1:2 * j + 2])
                 for j in range(SSD_HEADS // 2)]
        return jnp.concatenate(tiles, axis=1)

    acs_x = expand_heads(acs)
    dt_x = expand_heads(dt)
    filler()

    for g in range(SSD_GROUPS):
        gs = slice(g * gw, (g + 1) * gw)
        xs_g = xs[g]
        xdt = xs_g * dt_x[:, gs]
        xdt_b = xdt.astype(BF16)
        b_g = bcs[:, g * SSD_STATE:(g + 1) * SSD_STATE]
        c_g = bcs[:, (SSD_GROUPS + g) * SSD_STATE:(SSD_GROUPS + g + 1) * SSD_STATE]
        c_b = c_g.astype(BF16)
        cb = _dot_nt(c_b, b_g.astype(BF16))

        pieces = []
        for j in range(SSD_HPG // 2):
            filler()
            xp = xdt_b[:, j * V7X_LANES:(j + 1) * V7X_LANES]
            acc = None
            for half in range(2):
                h = g * SSD_HPG + 2 * j + half
                seg = acs[:, h:h + 1] - acs_t[h:h + 1, :]
                m = (cb * jnp.exp2(jnp.where(causal, seg, NEG_BIG))).astype(BF16)
                keep = first_half if half == 0 else jnp.logical_not(first_half)
                part = _dot(m, jnp.where(keep, xp, jnp.zeros_like(xp)))
                acc = part if acc is None else acc + part
            pieces.append(acc)
        y_diag = jnp.concatenate(pieces, axis=-1)

        st = state_ref[g]
        acs_g = acs_x[:, gs]
        y_off = _dot(c_b, st.astype(BF16)) * jnp.exp2(acs_g)
        y = y_diag + y_off + dskip_ref[:, gs] * xs_g

        last = acs_g[wc - 1:wc, :]
        xdec = (xdt * jnp.exp2(last - acs_g)).astype(BF16)
        state_ref[g] = st * jnp.exp2(last) + _dot(b_g.T.astype(BF16), xdec)

        yz = y * _silu(z[:, gs])
        o_ref[:, gs] = (_rms(yz) * nw_ref[:, gs]).astype(o_ref.dtype)


def _mixer_body(x_ref, prew_ref, w_ref, lbl_ref, hnw_ref, cw_ref, cb_ref, dtb_ref, alog_ref,
                dskip_ref, snw_ref, wg_ref, wu_ref, oa_ref, ob_ref, wgo_ref, wuo_ref,
                hst_ref, lvl_ref, lb_ref, xbuf_ref, sst_ref, pj_ref, wdt_ref, *, layer, chunks_per_seq):
    hw = HGRN_HALF
    s = pl.program_id(0)
    grp = HGRN_HEADS_PER_GROUP
    ssd_lo = HG_HEADS * HG_HEAD_COLS
    dt_lo = ssd_lo + SSD_COLS - V7X_LANES
    wgo_ref[...] = wg_ref[...].astype(wgo_ref.dtype)
    wuo_ref[...] = wu_ref[...].astype(wuo_ref.dtype)
    n_all = HG_HEADS * HG_HEAD_COLS + SSD_COLS
    n_in = w_ref.shape[1]

    @pl.when(s == 0)
    def _():
        pj_ref[...] = jnp.zeros_like(pj_ref)
        wdt_ref[...] = jnp.zeros_like(wdt_ref)
        wdt_ref[:, 0:n_in - dt_lo] = w_ref[:, dt_lo:n_in]
        t_idx = lax.broadcasted_iota(jnp.int32, (hw, hw), 0)
        s_idx = lax.broadcasted_iota(jnp.int32, (hw, hw), 1)
        lv = 32 - lax.clz(t_idx ^ s_idx)
        lvl_ref[...] = jnp.where(t_idx >= s_idx, lv, -1)
        lg = lbl_ref[...]
        e = jnp.exp(lg - jnp.max(lg, axis=0, keepdims=True))
        sm = e / jnp.sum(e, axis=0, keepdims=True)
        lb_ref[...] = jnp.sum(sm[: layer + 1], axis=0, keepdims=True)

    @pl.when(jnp.logical_or(s == 0, lax.rem(s - 1, chunks_per_seq) == 0))
    def _():
        hst_ref[...] = jnp.zeros_like(hst_ref)
        sst_ref[...] = jnp.zeros_like(sst_ref)
        xbuf_ref[:, 0:V7X_SUBLANES, :] = jnp.zeros((xbuf_ref.shape[0], V7X_SUBLANES, V7X_LANES), F32)

    hb = (_rms(x_ref[...]) * prew_ref[...]).astype(BF16)
    todo = [(lo, min(MIX_PIECE_COLS, n_all - lo)) for lo in range(0, n_all, MIX_PIECE_COLS)]
    new_proj = []

    def weight_cols(lo, width):
        blocks = []
        for j in range(lo // V7X_LANES, (lo + width) // V7X_LANES):
            src = j if j >= ssd_lo // V7X_LANES else (j % 4) * HG_HEADS + j // 4
            if src * V7X_LANES == dt_lo:
                blocks.append(wdt_ref[...])
            else:
                blocks.append(w_ref[:, src * V7X_LANES:(src + 1) * V7X_LANES])
        return blocks[0] if len(blocks) == 1 else jnp.concatenate(blocks, axis=1)

    def filler():
        if todo:
            lo, width = todo.pop(0)
            new_proj.append((lo, width, _dot(hb, weight_cols(lo, width))))

    _ssd_chunk(pj_ref[:, ssd_lo:], cw_ref, cb_ref, dtb_ref, alog_ref, dskip_ref, snw_ref, ob_ref,
               xbuf_ref, sst_ref, filler)
    for gi in range(HG_HEADS // grp):
        heads = []
        for g in range(grp):
            h = gi * grp + g
            base = h * HG_HEAD_COLS
            ks = slice(h * HG_DK, (h + 1) * HG_DK)
            vs = slice(h * HG_DV, (h + 1) * HG_DV)
            heads.append((pj_ref[:, base:base + HG_DK], pj_ref[:, base + HG_DK:base + 2 * HG_DK],
                          pj_ref[:, base + 2 * HG_DK:base + 2 * HG_DK + HG_DV],
                          pj_ref[:, base + 2 * HG_DK + HG_DV:base + HG_HEAD_COLS],
                          lb_ref[:, ks], hnw_ref[:, vs], hst_ref[h]))
        for g, (out, st_new) in enumerate(_hgrn_group(heads, lvl_ref, filler)):
            h = gi * grp + g
            hst_ref[h] = st_new
            oa_ref[:, h * HG_DV:(h + 1) * HG_DV] = out
    while todo:
        filler()

    for lo, width, val in new_proj:
        pj_ref[:, lo:lo + width] = val


def _mixer(x2, pre_w, w_all, lb_logits, hgrn_nw, conv_w, conv_b, dt_bias_p, a_log_p, d_skip_x,
           ssd_nw, w_gate, w_up, seqlen, layer):
    t, d = x2.shape
    wc = MIX_CHUNK
    n_chunks = t // wc
    f = w_gate.shape[1]
    wf_rows = d // n_chunks
    assert wf_rows * n_chunks == d and wf_rows % (2 * V7X_SUBLANES) == 0
    n_all = HG_HEADS * HG_HEAD_COLS + SSD_COLS
    assert w_all.shape[1] == n_all - V7X_LANES + SSD_HEADS
    xbuf_shape = ((SSD_WIDTH + SSD_BC_WIDTH) // V7X_LANES, V7X_SUBLANES + wc, V7X_LANES)
    blocks = (_nbytes((wc, d), F32) + _nbytes((wc, HG_VAL), MIX_DTYPE) + _nbytes((wc, SSD_WIDTH), MIX_DTYPE)
              + 2 * _nbytes((wf_rows, f), F32) + 2 * _nbytes((wf_rows, f), BF16))
    resident = _nbytes(w_all.shape, BF16)
    scratch = (_nbytes((HG_HEADS, HG_DV, HG_DK), F32) + _nbytes((HGRN_HALF, HGRN_HALF), jnp.int32)
               + _nbytes((V7X_SUBLANES, HG_KEY), F32) + _nbytes(xbuf_shape, F32)
               + _nbytes((SSD_GROUPS, SSD_STATE, SSD_GROUP_WIDTH), F32) + _nbytes((wc, n_all), F32))
    values = _nbytes((wc, d), BF16) + _nbytes((wc, n_all), F32) + 8 * _nbytes((wc, SSD_WIDTH), F32)

    def whole(arr, **kw):
        return pl.BlockSpec(arr.shape, lambda s: (0,) * arr.ndim, **kw)

    def finished(width):
        return pl.BlockSpec((wc, width), lambda s: (jnp.maximum(s - 1, 0), 0))

    wf_spec = pl.BlockSpec((wf_rows, f), lambda s: (jnp.minimum(s, n_chunks - 1), 0))

    return pl.pallas_call(
        functools.partial(_mixer_body, layer=layer, chunks_per_seq=seqlen // wc),
        grid=(n_chunks + 1,),
        in_specs=[
            pl.BlockSpec((wc, d), lambda s: (jnp.minimum(s, n_chunks - 1), 0)),
            whole(pre_w),
            whole(w_all, pipeline_mode=pl.Buffered(1)),
            whole(lb_logits), whole(hgrn_nw), whole(conv_w), whole(conv_b), whole(dt_bias_p),
            whole(a_log_p), whole(d_skip_x), whole(ssd_nw), wf_spec, wf_spec,
        ],
        out_specs=[finished(HG_VAL), finished(SSD_WIDTH), wf_spec, wf_spec],
        out_shape=[jax.ShapeDtypeStruct((t, HG_VAL), MIX_DTYPE),
                   jax.ShapeDtypeStruct((t, SSD_WIDTH), MIX_DTYPE),
                   jax.ShapeDtypeStruct(w_gate.shape, BF16), jax.ShapeDtypeStruct(w_up.shape, BF16)],
        scratch_shapes=[
            pltpu.VMEM((HG_HEADS, HG_DV, HG_DK), F32),
            pltpu.VMEM((HGRN_HALF, HGRN_HALF), jnp.int32),
            pltpu.VMEM((1, HG_KEY), F32),
            pltpu.VMEM(xbuf_shape, F32),
            pltpu.VMEM((SSD_GROUPS, SSD_STATE, SSD_GROUP_WIDTH), F32),
            pltpu.VMEM((wc, n_all), F32),
            pltpu.VMEM((d, V7X_LANES), BF16),
        ],
        compiler_params=pltpu.CompilerParams(
            dimension_semantics=("arbitrary",),
            vmem_limit_bytes=_vmem_limit(blocks, resident + scratch, values),
        ),
        name="mixer",
    )(x2, pre_w, w_all, lb_logits, hgrn_nw, conv_w, conv_b, dt_bias_p, a_log_p, d_skip_x, ssd_nw,
      w_gate, w_up)


def _outproj_body(oa_ref, ob_ref, wa_ref, wb_ref, x_ref, postw_ref, prew_ref, wd_ref,
                  x1_ref, h2_ref, wdo_ref):
    wdo_ref[...] = wd_ref[...].astype(wdo_ref.dtype)
    for r in range(x_ref.shape[0] // OUTPROJ_SUB):
        rs = slice(r * OUTPROJ_SUB, (r + 1) * OUTPROJ_SUB)
        mix = _dot(oa_ref[rs, :], wa_ref[...]) + _dot(ob_ref[rs, :], wb_ref[...])
        x1 = x_ref[rs, :] + _rms(mix) * postw_ref[...]
        x1_ref[rs, :] = x1
        h2_ref[rs, :] = (_rms(x1) * prew_ref[...]).astype(h2_ref.dtype)


def _out_proj(o_a, o_b, w_a, w_b, x2, post_w, pre_w, w_down):
    t, d = x2.shape
    tm = OUTPROJ_TM
    wd_rows = w_down.shape[0] // (t // tm)
    assert wd_rows * (t // tm) == w_down.shape[0] and wd_rows % (2 * V7X_SUBLANES) == 0
    blocks = (_nbytes((tm, HG_VAL), MIX_DTYPE) + _nbytes((tm, SSD_WIDTH), MIX_DTYPE)
              + _nbytes(w_a.shape, BF16) + _nbytes(w_b.shape, BF16) + 2 * _nbytes((tm, d), F32)
              + _nbytes((tm, d), BF16) + _nbytes((wd_rows, d), F32) + _nbytes((wd_rows, d), BF16))

    def row(i):
        return (i, 0)

    def fixed(i):
        return (0, 0)

    return pl.pallas_call(
        _outproj_body,
        grid=(t // tm,),
        in_specs=[
            pl.BlockSpec((tm, HG_VAL), row),
            pl.BlockSpec((tm, SSD_WIDTH), row),
            pl.BlockSpec(w_a.shape, fixed),
            pl.BlockSpec(w_b.shape, fixed),
            pl.BlockSpec((tm, d), row),
            pl.BlockSpec((1, d), fixed),
            pl.BlockSpec((1, d), fixed),
            pl.BlockSpec((wd_rows, d), row),
        ],
        out_specs=[pl.BlockSpec((tm, d), row), pl.BlockSpec((tm, d), row),
                   pl.BlockSpec((wd_rows, d), row)],
        out_shape=[jax.ShapeDtypeStruct((t, d), F32), jax.ShapeDtypeStruct((t, d), BF16),
                   jax.ShapeDtypeStruct(w_down.shape, BF16)],
        compiler_params=pltpu.CompilerParams(
            dimension_semantics=("parallel",),
            vmem_limit_bytes=_vmem_limit(blocks),
        ),
        name="out_proj",
    )(o_a, o_b, w_a, w_b, x2, post_w, pre_w, w_down)


def _ffn_body(h_ref, x1c_ref, wg_ref, wu_ref, wd_ref, postw_ref, o_ref, x1_ref):
    j = pl.program_id(1)
    d = o_ref.shape[1]
    xc = x1c_ref.shape[1]

    @pl.when(j == 0)
    def _():
        o_ref[...] = jnp.zeros_like(o_ref)

    @pl.when(j < d // xc)
    def _():
        x1_ref[:, pl.ds(pl.multiple_of(j * xc, xc), xc)] = x1c_ref[...]

    h = h_ref[...]
    hids = []
    for s in range(FFN_TF // FFN_SUB):
        cs = slice(s * FFN_SUB, (s + 1) * FFN_SUB)
        g = _dot(h, wg_ref[:, cs])
        u = _dot(h, wu_ref[:, cs])
        hids.append((_silu(g) * u).astype(BF16))
    hid = jnp.concatenate(hids, axis=1)
    for c in range(d // FFN_DOWN_COLS):
        cs = slice(c * FFN_DOWN_COLS, (c + 1) * FFN_DOWN_COLS)
        o_ref[:, cs] += _dot(hid, wd_ref[:, cs])

    @pl.when(j == pl.num_programs(1) - 1)
    def _():
        o_ref[...] = x1_ref[...] + _rms(o_ref[...]) * postw_ref[...]


def _ffn(h2, x1, w_gate, w_up, w_down, post_w):
    t, d = x1.shape
    f = w_gate.shape[1]
    tm, tf, xc = FFN_TM, FFN_TF, FFN_X1_COLS
    n_xc = d // xc
    assert f // tf >= n_xc
    blocks = (_nbytes((tm, d), BF16) + _nbytes((tm, xc), F32) + 2 * _nbytes((d, tf), BF16)
              + _nbytes((tf, d), BF16) + _nbytes((tm, d), F32))
    values = (2 * _nbytes((tm, tf), F32) + _nbytes((tm, tf), BF16) + _nbytes((tm, d), F32))
    return pl.pallas_call(
        _ffn_body,
        grid=(t // tm, f // tf),
        in_specs=[
            pl.BlockSpec((tm, d), lambda i, j: (i, 0)),
            pl.BlockSpec((tm, xc), lambda i, j: (i, jnp.minimum(j, n_xc - 1))),
            pl.BlockSpec((d, tf), lambda i, j: (0, j)),
            pl.BlockSpec((d, tf), lambda i, j: (0, j)),
            pl.BlockSpec((tf, d), lambda i, j: (j, 0)),
            pl.BlockSpec((1, d), lambda i, j: (0, 0)),
        ],
        out_specs=pl.BlockSpec((tm, d), lambda i, j: (i, 0)),
        out_shape=jax.ShapeDtypeStruct((t, d), F32),
        scratch_shapes=[pltpu.VMEM((tm, d), F32)],
        compiler_params=pltpu.CompilerParams(
            dimension_semantics=("parallel", "arbitrary"),
            vmem_limit_bytes=_vmem_limit(blocks, _nbytes((tm, d), F32), values),
        ),
        name="ffn",
    )(h2, x1, w_gate, w_up, w_down, post_w)


def _pad_lanes(v):
    return jnp.pad(v.astype(F32), (0, V7X_LANES - v.shape[0]))[None, :]


def kernel(x, pre_mix_norm_w, w_in, lb_logits, conv_w, conv_b, dt_bias, a_log, d_skip, hgrn_norm_w,
           ssd_norm_w, w_out, post_mix_norm_w, pre_ffn_norm_w, w_gate, w_up, w_down, post_ffn_norm_w):
    batch, seqlen, d = x.shape
    depth = w_in.shape[0]

    x2 = x.reshape(batch * seqlen, d)
    for l in range(depth):
        o_a, o_b, wg_b, wu_b = _mixer(
            x2, pre_mix_norm_w[l][None, :], w_in[l].astype(BF16), lb_logits.astype(F32),
            hgrn_norm_w[l][None, :], conv_w[l], conv_b[l][None, :], _pad_lanes(dt_bias[l]),
            _pad_lanes(a_log[l]), jnp.repeat(d_skip[l].astype(F32), SSD_HEADDIM)[None, :],
            ssd_norm_w[l][None, :], w_gate[l], w_up[l], seqlen, l)
        x1, h2, wd_b = _out_proj(o_a, o_b, w_out[l, :HG_VAL].astype(BF16), w_out[l, HG_VAL:].astype(BF16),
                                 x2, post_mix_norm_w[l][None, :], pre_ffn_norm_w[l][None, :], w_down[l])
        x2 = _ffn(h2, x1, wg_b, wu_b, wd_b, post_ffn_norm_w[l][None, :])
    return x2.reshape(batch, seqlen, d)
```

```python
import functools

import jax
import jax.numpy as jnp
from jax import lax
from jax.experimental import pallas as pl
from jax.experimental.pallas import tpu as pltpu

F32 = jnp.float32
BF16 = jnp.bfloat16

HG_HEADS = 8
HG_DK = 128
HG_DV = 128
HG_KEY = HG_HEADS * HG_DK
HG_VAL = HG_HEADS * HG_DV
SSD_HEADS = 16
SSD_HEADDIM = 64
SSD_WIDTH = SSD_HEADS * SSD_HEADDIM
SSD_GROUPS = 2
SSD_HPG = SSD_HEADS // SSD_GROUPS
SSD_STATE = 128
SSD_CONV = 4
SSD_GROUP_WIDTH = SSD_HPG * SSD_HEADDIM
SSD_BC_WIDTH = 2 * SSD_GROUPS * SSD_STATE
NORM_EPS = 1e-6

V7X_LANES = 128
V7X_SUBLANES = 8
V7X_VMEM_BYTES = 64 * 1024 * 1024
V7X_VMEM_COMPILER_RESERVE = 6 * 1024 * 1024
V7X_VMEM_UNSCOPED = 2 * 1024 * 1024

MIX_DTYPE = BF16
MIX_CHUNK = 256
HGRN_HALF = 128
HGRN_HEADS_PER_GROUP = 4
MIX_PIECE_COLS = 256
HG_HEAD_COLS = 2 * HG_DK + 2 * HG_DV
SSD_COLS = SSD_WIDTH + SSD_WIDTH + SSD_BC_WIDTH + V7X_LANES
OUTPROJ_TM = 512
OUTPROJ_SUB = 256
FFN_TM = 1024
FFN_TF = 512
FFN_SUB = 256
FFN_DOWN_COLS = 512
FFN_X1_COLS = 256
NEG_BIG = -1e30
LOG2_E = 1.4426950408889634


def _nbytes(shape, dtype):
    n = 1
    for s in shape:
        n *= s
    return n * jnp.dtype(dtype).itemsize


def _vmem_limit(block_bytes, scratch_bytes=0, value_bytes=0):
    need = 2 * block_bytes + scratch_bytes + value_bytes + V7X_VMEM_COMPILER_RESERVE
    return int(min(need, V7X_VMEM_BYTES - V7X_VMEM_UNSCOPED))


def _sigmoid(x):
    return 1.0 / (1.0 + jnp.exp2(x * (-LOG2_E)))


def _silu(x):
    return x * _sigmoid(x)


def _rms(x):
    return x * lax.rsqrt(jnp.mean(x * x, axis=-1, keepdims=True) + NORM_EPS)


def _dot(a, b):
    return jnp.dot(a, b, preferred_element_type=F32)


def _dot_nt(a, b):
    return lax.dot_general(a, b, (((1,), (1,)), ((), ())), preferred_element_type=F32)


def _split2(x, p):
    x4 = x.reshape(x.shape[0] // (2 * p), 2, p, x.shape[1])
    return x4[:, 0], x4[:, 1]


def _merge2(lo, hi):
    return jnp.stack([lo, hi], axis=1).reshape(-1, lo.shape[-1])


def _hgrn_group(heads, lvl_ref, filler):
    g = len(heads)
    w = MIX_CHUNK
    hw = HGRN_HALF
    sub = V7X_SUBLANES
    rows = g * w
    n_half = rows // hw
    assert w == 2 * hw and hw == V7X_LANES
    shp3 = (rows // sub, sub, HG_DK)
    sub_idx = lax.broadcasted_iota(jnp.int32, (1, sub, HG_DK), 1)

    q = jnp.concatenate([_silu(h[0]) * (HG_DK ** -0.5) for h in heads], axis=0)
    f = jnp.concatenate([h[4] + (1.0 - h[4]) * _sigmoid(h[1]) for h in heads], axis=0)
    k = 1.0 - f

    def half_scores(qn, kn, level, a_blocks):
        qb, kb = qn.astype(BF16), kn.astype(BF16)
        out = []
        for c in range(n_half):
            rs = slice(c * hw, (c + 1) * hw)
            s_c = _dot_nt(qb[rs], kb[rs])
            keep = lvl_ref[...] == level
            out.append(jnp.where(keep, s_c, 0.0 if a_blocks is None else a_blocks[c]))
        return out

    a_diag = half_scores(q, k, 0, None)
    filler()
    cq = f.reshape(shp3)
    ck = jnp.ones(shp3, F32)
    r = cq
    q3, k3 = q.reshape(shp3), k.reshape(shp3)
    level = 1
    p = 1
    while p < sub:
        a_diag = half_scores((q3 * cq).reshape(rows, HG_DK), (k3 * ck).reshape(rows, HG_DK), level, a_diag)
        upper = (sub_idx & p) != 0
        down = pltpu.roll(r, p, axis=1)
        up = pltpu.roll(r, sub - p, axis=1)
        cq = cq * jnp.where(upper, down, 1.0)
        ck = ck * jnp.where(upper, 1.0, up)
        r = r * jnp.where(upper, down, up)
        filler()
        p *= 2
        level += 1
    cq = cq.reshape(rows, HG_DK)
    ck = ck.reshape(rows, HG_DK)

    a_cross = None
    while p < w:
        cq_lo, cq_hi = _split2(cq, p)
        ck_lo, ck_hi = _split2(ck, p)
        q_lo, q_hi = _split2(q, p)
        k_lo, k_hi = _split2(k, p)
        nb = rows // (2 * p)
        if p < hw:
            qu = (q_hi * cq_hi).reshape(rows // 2, HG_DK).astype(BF16)
            kn = _merge2(k_lo * ck_lo, k_hi).astype(BF16)
            new_a = []
            for c in range(n_half):
                s_c = _dot_nt(qu[c * (hw // 2):(c + 1) * (hw // 2)], kn[c * hw:(c + 1) * hw])
                a_lo, a_hi = _split2(a_diag[c], p)
                _, lv_hi = _split2(lvl_ref[...], p)
                a_hi = jnp.where(lv_hi == level, s_c.reshape(a_hi.shape), a_hi)
                new_a.append(_merge2(a_lo, a_hi))
            a_diag = new_a
        else:
            qu = (q_hi * cq_hi).astype(BF16)
            kl = (k_lo * ck_lo).astype(BF16)
            a_cross = [_dot_nt(qu[i], kl[i]) for i in range(g)]
        r4 = r.reshape(nb, 2, sub, HG_DK)
        r_lo, r_hi = r4[:, 0], r4[:, 1]
        cq_hi = (cq_hi.reshape(nb, p // sub, sub, HG_DK) * r_lo[:, None]).reshape(nb, p, HG_DK)
        ck_lo = (ck_lo.reshape(nb, p // sub, sub, HG_DK) * r_hi[:, None]).reshape(nb, p, HG_DK)
        cq = _merge2(cq_lo, cq_hi)
        ck = _merge2(ck_lo, ck_hi)
        r = r_lo * r_hi
        filler()
        p *= 2
        level += 1

    qn = (q * cq).astype(BF16)
    kn = (k * ck).astype(BF16)
    results = []
    for i, (_, _, v, gate, _, nw, st) in enumerate(heads):
        rs = slice(i * w, (i + 1) * w)
        vb = v.astype(BF16)
        o_inter = _dot_nt(qn[rs], st.astype(BF16))
        o0 = _dot(a_diag[2 * i].astype(BF16), vb[:hw])
        o1 = _dot(jnp.concatenate([a_cross[i], a_diag[2 * i + 1]], axis=1).astype(BF16), vb)
        o = jnp.concatenate([o0, o1], axis=0) + o_inter
        st_new = st * r[i, 0:1, :] + _dot(v.T.astype(BF16), kn[rs])
        results.append(((_rms(o) * nw * _silu(gate)).astype(MIX_DTYPE), st_new))
    return results


def _split3(x):
    hi = x.astype(BF16)
    r1 = x - hi.astype(F32)
    mid = r1.astype(BF16)
    lo = (r1 - mid.astype(F32)).astype(BF16)
    return hi, mid, lo


def _dot_exact01_left(m01, x):
    hi, mid, lo = _split3(x)
    return _dot(m01, hi) + _dot(m01, mid) + _dot(m01, lo)


def _ssd_chunk(ps, cw_ref, cb_ref, dtb_ref, alog_ref, dskip_ref, nw_ref, o_ref, xbuf_ref, state_ref,
               filler):
    wc = MIX_CHUNK
    hdr = V7X_SUBLANES
    gw = SSD_GROUP_WIDTH
    z = ps[:, :SSD_WIDTH]
    xbc = ps[:, SSD_WIDTH:SSD_WIDTH + SSD_WIDTH + SSD_BC_WIDTH]
    dt_raw = ps[:, SSD_WIDTH + SSD_WIDTH + SSD_BC_WIDTH:]

    n_slabs = xbc.shape[1] // V7X_LANES
    cols = []
    for si in range(n_slabs):
        cs = slice(si * V7X_LANES, (si + 1) * V7X_LANES)
        xbuf_ref[si, hdr:hdr + wc, :] = xbc[:, cs]
        acc = cb_ref[:, cs]
        for kk in range(SSD_CONV):
            off = hdr - (SSD_CONV - 1) + kk
            acc = acc + cw_ref[kk:kk + 1, cs] * xbuf_ref[si, off:off + wc, :]
        xbuf_ref[si, 0:hdr, :] = xbuf_ref[si, wc:wc + hdr, :]
        cols.append(_silu(acc))
    per_group = gw // V7X_LANES
    xs = [jnp.concatenate(cols[g * per_group:(g + 1) * per_group], axis=1) for g in range(SSD_GROUPS)]
    bcs = jnp.concatenate(cols[SSD_GROUPS * per_group:], axis=1)
    filler()

    dtr = dt_raw + dtb_ref[...]
    dt = jnp.maximum(dtr, 0.0) + jnp.log1p(jnp.exp(-jnp.abs(dtr)))
    a = dt * (-jnp.exp(alog_ref[...]))
    t_idx = lax.broadcasted_iota(jnp.int32, (wc, wc), 0)
    s_idx = lax.broadcasted_iota(jnp.int32, (wc, wc), 1)
    causal = t_idx >= s_idx
    tril01 = jnp.where(causal, 1.0, 0.0).astype(BF16)
    acs = _dot_exact01_left(tril01, a) * LOG2_E
    acs_t = acs.T
    filler()

    lane = lax.broadcasted_iota(jnp.int32, (wc, V7X_LANES), 1)
    first_half = lane < SSD_HEADDIM

    def expand_heads(v):
        tiles = [jnp.where(first_half, v[:, 2 * j:2 * j + 1], v[:, 2 * j + 1:2 * j + 2])
                 for j in range(SSD_HEADS // 2)]
        return jnp.concatenate(tiles, axis=1)

    acs_x = expand_heads(acs)
    dt_x = expand_heads(dt)
    filler()

    for g in range(SSD_GROUPS):
        gs = slice(g * gw, (g + 1) * gw)
        xs_g = xs[g]
        xdt = xs_g * dt_x[:, gs]
        xdt_b = xdt.astype(BF16)
        b_g = bcs[:, g * SSD_STATE:(g + 1) * SSD_STATE]
        c_g = bcs[:, (SSD_GROUPS + g) * SSD_STATE:(SSD_GROUPS + g + 1) * SSD_STATE]
        c_b = c_g.astype(BF16)
        cb = _dot_nt(c_b, b_g.astype(BF16))

        pieces = []
        for j in range(SSD_HPG // 2):
            filler()
            xp = xdt_b[:, j * V7X_LANES:(j + 1) * V7X_LANES]
            acc = None
            for half in range(2):
                h = g * SSD_HPG + 2 * j + half
                seg = acs[:, h:h + 1] - acs_t[h:h + 1, :]
                m = (cb * jnp.exp2(jnp.where(causal, seg, NEG_BIG))).astype(BF16)
                keep = first_half if half == 0 else jnp.logical_not(first_half)
                part = _dot(m, jnp.where(keep, xp, jnp.zeros_like(xp)))
                acc = part if acc is None else acc + part
            pieces.append(acc)
        y_diag = jnp.concatenate(pieces, axis=-1)

        st = state_ref[g]
        acs_g = acs_x[:, gs]
        y_off = _dot(c_b, st.astype(BF16)) * jnp.exp2(acs_g)
        y = y_diag + y_off + dskip_ref[:, gs] * xs_g

        last = acs_g[wc - 1:wc, :]
        xdec = (xdt * jnp.exp2(last - acs_g)).astype(BF16)
        state_ref[g] = st * jnp.exp2(last) + _dot(b_g.T.astype(BF16), xdec)

        yz = y * _silu(z[:, gs])
        o_ref[:, gs] = (_rms(yz) * nw_ref[:, gs]).astype(o_ref.dtype)


def _mixer_body(x_ref, prew_ref, w_ref, lbl_ref, hnw_ref, cw_ref, cb_ref, dtb_ref, alog_ref,
                dskip_ref, snw_ref, wg_ref, wu_ref, oa_ref, ob_ref, wgo_ref, wuo_ref,
                hst_ref, lvl_ref, lb_ref, xbuf_ref, sst_ref, pj_ref, wdt_ref, *, layer, chunks_per_seq):
    hw = HGRN_HALF
    s = pl.program_id(0)
    grp = HGRN_HEADS_PER_GROUP
    ssd_lo = HG_HEADS * HG_HEAD_COLS
    dt_lo = ssd_lo + SSD_COLS - V7X_LANES
    wgo_ref[...] = wg_ref[...].astype(wgo_ref.dtype)
    wuo_ref[...] = wu_ref[...].astype(wuo_ref.dtype)
    n_all = HG_HEADS * HG_HEAD_COLS + SSD_COLS
    n_in = w_ref.shape[1]

    @pl.when(s == 0)
    def _():
        pj_ref[...] = jnp.zeros_like(pj_ref)
        wdt_ref[...] = jnp.zeros_like(wdt_ref)
        wdt_ref[:, 0:n_in - dt_lo] = w_ref[:, dt_lo:n_in]
        t_idx = lax.broadcasted_iota(jnp.int32, (hw, hw), 0)
        s_idx = lax.broadcasted_iota(jnp.int32, (hw, hw), 1)
        lv = 32 - lax.clz(t_idx ^ s_idx)
        lvl_ref[...] = jnp.where(t_idx >= s_idx, lv, -1)
        lg = lbl_ref[...]
        e = jnp.exp(lg - jnp.max(lg, axis=0, keepdims=True))
        sm = e / jnp.sum(e, axis=0, keepdims=True)
        lb_ref[...] = jnp.sum(sm[: layer + 1], axis=0, keepdims=True)

    @pl.when(jnp.logical_or(s == 0, lax.rem(s - 1, chunks_per_seq) == 0))
    def _():
        hst_ref[...] = jnp.zeros_like(hst_ref)
        sst_ref[...] = jnp.zeros_like(sst_ref)
        xbuf_ref[:, 0:V7X_SUBLANES, :] = jnp.zeros((xbuf_ref.shape[0], V7X_SUBLANES, V7X_LANES), F32)

    hb = (_rms(x_ref[...]) * prew_ref[...]).astype(BF16)
    todo = [(lo, min(MIX_PIECE_COLS, n_all - lo)) for lo in range(0, n_all, MIX_PIECE_COLS)]
    new_proj = []

    def weight_cols(lo, width):
        blocks = []
        for j in range(lo // V7X_LANES, (lo + width) // V7X_LANES):
            src = j if j >= ssd_lo // V7X_LANES else (j % 4) * HG_HEADS + j // 4
            if src * V7X_LANES == dt_lo:
                blocks.append(wdt_ref[...])
            else:
                blocks.append(w_ref[:, src * V7X_LANES:(src + 1) * V7X_LANES])
        return blocks[0] if len(blocks) == 1 else jnp.concatenate(blocks, axis=1)

    def filler():
        if todo:
            lo, width = todo.pop(0)
            new_proj.append((lo, width, _dot(hb, weight_cols(lo, width))))

    _ssd_chunk(pj_ref[:, ssd_lo:], cw_ref, cb_ref, dtb_ref, alog_ref, dskip_ref, snw_ref, ob_ref,
               xbuf_ref, sst_ref, filler)
    for gi in range(HG_HEADS // grp):
        heads = []
        for g in range(grp):
            h = gi * grp + g
            base = h * HG_HEAD_COLS
            ks = slice(h * HG_DK, (h + 1) * HG_DK)
            vs = slice(h * HG_DV, (h + 1) * HG_DV)
            heads.append((pj_ref[:, base:base + HG_DK], pj_ref[:, base + HG_DK:base + 2 * HG_DK],
                          pj_ref[:, base + 2 * HG_DK:base + 2 * HG_DK + HG_DV],
                          pj_ref[:, base + 2 * HG_DK + HG_DV:base + HG_HEAD_COLS],
                          lb_ref[:, ks], hnw_ref[:, vs], hst_ref[h]))
        for g, (out, st_new) in enumerate(_hgrn_group(heads, lvl_ref, filler)):
            h = gi * grp + g
            hst_ref[h] = st_new
            oa_ref[:, h * HG_DV:(h + 1) * HG_DV] = out
    while todo:
        filler()

    for lo, width, val in new_proj:
        pj_ref[:, lo:lo + width] = val


def _mixer(x2, pre_w, w_all, lb_logits, hgrn_nw, conv_w, conv_b, dt_bias_p, a_log_p, d_skip_x,
           ssd_nw, w_gate, w_up, seqlen, layer):
    t, d = x2.shape
    wc = MIX_CHUNK
    n_chunks = t // wc
    f = w_gate.shape[1]
    wf_rows = d // n_chunks
    assert wf_rows * n_chunks == d and wf_rows % (2 * V7X_SUBLANES) == 0
    n_all = HG_HEADS * HG_HEAD_COLS + SSD_COLS
    assert w_all.shape[1] == n_all - V7X_LANES + SSD_HEADS
    xbuf_shape = ((SSD_WIDTH + SSD_BC_WIDTH) // V7X_LANES, V7X_SUBLANES + wc, V7X_LANES)
    blocks = (_nbytes((wc, d), F32) + _nbytes((wc, HG_VAL), MIX_DTYPE) + _nbytes((wc, SSD_WIDTH), MIX_DTYPE)
              + 2 * _nbytes((wf_rows, f), F32) + 2 * _nbytes((wf_rows, f), BF16))
    resident = _nbytes(w_all.shape, BF16)
    scratch = (_nbytes((HG_HEADS, HG_DV, HG_DK), F32) + _nbytes((HGRN_HALF, HGRN_HALF), jnp.int32)
               + _nbytes((V7X_SUBLANES, HG_KEY), F32) + _nbytes(xbuf_shape, F32)
               + _nbytes((SSD_GROUPS, SSD_STATE, SSD_GROUP_WIDTH), F32) + _nbytes((wc, n_all), F32))
    values = _nbytes((wc, d), BF16) + _nbytes((wc, n_all), F32) + 8 * _nbytes((wc, SSD_WIDTH), F32)

    def whole(arr, **kw):
        return pl.BlockSpec(arr.shape, lambda s: (0,) * arr.ndim, **kw)

    def finished(width):
        return pl.BlockSpec((wc, width), lambda s: (jnp.maximum(s - 1, 0), 0))

    wf_spec = pl.BlockSpec((wf_rows, f), lambda s: (jnp.minimum(s, n_chunks - 1), 0))

    return pl.pallas_call(
        functools.partial(_mixer_body, layer=layer, chunks_per_seq=seqlen // wc),
        grid=(n_chunks + 1,),
        in_specs=[
            pl.BlockSpec((wc, d), lambda s: (jnp.minimum(s, n_chunks - 1), 0)),
            whole(pre_w),
            whole(w_all, pipeline_mode=pl.Buffered(1)),
            whole(lb_logits), whole(hgrn_nw), whole(conv_w), whole(conv_b), whole(dt_bias_p),
            whole(a_log_p), whole(d_skip_x), whole(ssd_nw), wf_spec, wf_spec,
        ],
        out_specs=[finished(HG_VAL), finished(SSD_WIDTH), wf_spec, wf_spec],
        out_shape=[jax.ShapeDtypeStruct((t, HG_VAL), MIX_DTYPE),
                   jax.ShapeDtypeStruct((t, SSD_WIDTH), MIX_DTYPE),
                   jax.ShapeDtypeStruct(w_gate.shape, BF16), jax.ShapeDtypeStruct(w_up.shape, BF16)],
        scratch_shapes=[
            pltpu.VMEM((HG_HEADS, HG_DV, HG_DK), F32),
            pltpu.VMEM((HGRN_HALF, HGRN_HALF), jnp.int32),
            pltpu.VMEM((1, HG_KEY), F32),
            pltpu.VMEM(xbuf_shape, F32),
            pltpu.VMEM((SSD_GROUPS, SSD_STATE, SSD_GROUP_WIDTH), F32),
            pltpu.VMEM((wc, n_all), F32),
            pltpu.VMEM((d, V7X_LANES), BF16),
        ],
        compiler_params=pltpu.CompilerParams(
            dimension_semantics=("arbitrary",),
            vmem_limit_bytes=_vmem_limit(blocks, resident + scratch, values),
        ),
        name="mixer",
    )(x2, pre_w, w_all, lb_logits, hgrn_nw, conv_w, conv_b, dt_bias_p, a_log_p, d_skip_x, ssd_nw,
      w_gate, w_up)


def _outproj_body(oa_ref, ob_ref, wa_ref, wb_ref, x_ref, postw_ref, prew_ref, wd_ref,
                  x1_ref, h2_ref, wdo_ref):
    wdo_ref[...] = wd_ref[...].astype(wdo_ref.dtype)
    for r in range(x_ref.shape[0] // OUTPROJ_SUB):
        rs = slice(r * OUTPROJ_SUB, (r + 1) * OUTPROJ_SUB)
        mix = _dot(oa_ref[rs, :], wa_ref[...]) + _dot(ob_ref[rs, :], wb_ref[...])
        x1 = x_ref[rs, :] + _rms(mix) * postw_ref[...]
        x1_ref[rs, :] = x1
        h2_ref[rs, :] = (_rms(x1) * prew_ref[...]).astype(h2_ref.dtype)


def _out_proj(o_a, o_b, w_a, w_b, x2, post_w, pre_w, w_down):
    t, d = x2.shape
    tm = OUTPROJ_TM
    wd_rows = w_down.shape[0] // (t // tm)
    assert wd_rows * (t // tm) == w_down.shape[0] and wd_rows % (2 * V7X_SUBLANES) == 0
    blocks = (_nbytes((tm, HG_VAL), MIX_DTYPE) + _nbytes((tm, SSD_WIDTH), MIX_DTYPE)
              + _nbytes(w_a.shape, BF16) + _nbytes(w_b.shape, BF16) + 2 * _nbytes((tm, d), F32)
              + _nbytes((tm, d), BF16) + _nbytes((wd_rows, d), F32) + _nbytes((wd_rows, d), BF16))

    def row(i):
        return (i, 0)

    def fixed(i):
        return (0, 0)

    return pl.pallas_call(
        _outproj_body,
        grid=(t // tm,),
        in_specs=[
            pl.BlockSpec((tm, HG_VAL), row),
            pl.BlockSpec((tm, SSD_WIDTH), row),
            pl.BlockSpec(w_a.shape, fixed),
            pl.BlockSpec(w_b.shape, fixed),
            pl.BlockSpec((tm, d), row),
            pl.BlockSpec((1, d), fixed),
            pl.BlockSpec((1, d), fixed),
            pl.BlockSpec((wd_rows, d), row),
        ],
        out_specs=[pl.BlockSpec((tm, d), row), pl.BlockSpec((tm, d), row),
                   pl.BlockSpec((wd_rows, d), row)],
        out_shape=[jax.ShapeDtypeStruct((t, d), F32), jax.ShapeDtypeStruct((t, d), BF16),
                   jax.ShapeDtypeStruct(w_down.shape, BF16)],
        compiler_params=pltpu.CompilerParams(
            dimension_semantics=("parallel",),
            vmem_limit_bytes=_vmem_limit(blocks),
        ),
        name="out_proj",
    )(o_a, o_b, w_a, w_b, x2, post_w, pre_w, w_down)


def _ffn_body(h_ref, x1c_ref, wg_ref, wu_ref, wd_ref, postw_ref, o_ref, x1_ref):
    j = pl.program_id(1)
    d = o_ref.shape[1]
    xc = x1c_ref.shape[1]

    n_j = pl.num_programs(1)

    @pl.when(j < d // xc)
    def _():
        x1_ref[:, pl.ds(pl.multiple_of(j * xc, xc), xc)] = x1c_ref[...]

    def step(first, last):
        h = h_ref[...]
        hids = []
        for s in range(FFN_TF // FFN_SUB):
            cs = slice(s * FFN_SUB, (s + 1) * FFN_SUB)
            g = _dot(h, wg_ref[:, cs])
            u = _dot(h, wu_ref[:, cs])
            hids.append((_silu(g) * u).astype(BF16))
        hid = jnp.concatenate(hids, axis=1)
        ssq = None
        for c in range(d // FFN_DOWN_COLS):
            cs = slice(c * FFN_DOWN_COLS, (c + 1) * FFN_DOWN_COLS)
            part = _dot(hid, wd_ref[:, cs])
            acc = part if first else o_ref[:, cs] + part
            o_ref[:, cs] = acc
            if last:
                sq = jnp.sum(acc * acc, axis=-1, keepdims=True)
                ssq = sq if ssq is None else ssq + sq
        if last:
            scale = lax.rsqrt(ssq * (1.0 / d) + NORM_EPS)
            o_ref[...] = x1_ref[...] + o_ref[...] * scale * postw_ref[...]

    @pl.when(j == 0)
    def _():
        step(True, False)

    @pl.when(jnp.logical_and(j > 0, j < n_j - 1))
    def _():
        step(False, False)

    @pl.when(j == n_j - 1)
    def _():
        step(False, True)


def _ffn(h2, x1, w_gate, w_up, w_down, post_w):
    t, d = x1.shape
    f = w_gate.shape[1]
    tm, tf, xc = FFN_TM, FFN_TF, FFN_X1_COLS
    n_xc = d // xc
    assert f // tf >= n_xc
    blocks = (_nbytes((tm, d), BF16) + _nbytes((tm, xc), F32) + 2 * _nbytes((d, tf), BF16)
              + _nbytes((tf, d), BF16) + _nbytes((tm, d), F32))
    values = (2 * _nbytes((tm, tf), F32) + _nbytes((tm, tf), BF16) + _nbytes((tm, d), F32))
    return pl.pallas_call(
        _ffn_body,
        grid=(t // tm, f // tf),
        in_specs=[
            pl.BlockSpec((tm, d), lambda i, j: (i, 0)),
            pl.BlockSpec((tm, xc), lambda i, j: (i, jnp.minimum(j, n_xc - 1))),
            pl.BlockSpec((d, tf), lambda i, j: (0, j)),
            pl.BlockSpec((d, tf), lambda i, j: (0, j)),
            pl.BlockSpec((tf, d), lambda i, j: (j, 0)),
            pl.BlockSpec((1, d), lambda i, j: (0, 0)),
        ],
        out_specs=pl.BlockSpec((tm, d), lambda i, j: (i, 0)),
        out_shape=jax.ShapeDtypeStruct((t, d), F32),
        scratch_shapes=[pltpu.VMEM((tm, d), F32)],
        compiler_params=pltpu.CompilerParams(
            dimension_semantics=("parallel", "arbitrary"),
            vmem_limit_bytes=_vmem_limit(blocks, _nbytes((tm, d), F32), values),
        ),
        name="ffn",
    )(h2, x1, w_gate, w_up, w_down, post_w)


def _pad_lanes(v):
    return jnp.pad(v.astype(F32), (0, V7X_LANES - v.shape[0]))[None, :]


def kernel(x, pre_mix_norm_w, w_in, lb_logits, conv_w, conv_b, dt_bias, a_log, d_skip, hgrn_norm_w,
           ssd_norm_w, w_out, post_mix_norm_w, pre_ffn_norm_w, w_gate, w_up, w_down, post_ffn_norm_w):
    batch, seqlen, d = x.shape
    depth = w_in.shape[0]

    x2 = x.reshape(batch * seqlen, d)
    for l in range(depth):
        o_a, o_b, wg_b, wu_b = _mixer(
            x2, pre_mix_norm_w[l][None, :], w_in[l].astype(BF16), lb_logits.astype(F32),
            hgrn_norm_w[l][None, :], conv_w[l], conv_b[l][None, :], _pad_lanes(dt_bias[l]),
            _pad_lanes(a_log[l]), jnp.repeat(d_skip[l].astype(F32), SSD_HEADDIM)[None, :],
            ssd_norm_w[l][None, :], w_gate[l], w_up[l], seqlen, l)
        x1, h2, wd_b = _out_proj(o_a, o_b, w_out[l, :HG_VAL].astype(BF16), w_out[l, HG_VAL:].astype(BF16),
                                 x2, post_mix_norm_w[l][None, :], pre_ffn_norm_w[l][None, :], w_down[l])
        x2 = _ffn(h2, x1, wg_b, wu_b, wd_b, post_ffn_norm_w[l][None, :])
    return x2.reshape(batch, seqlen, d)
```

```python
import functools

import jax
import jax.numpy as jnp
from jax import lax
from jax.experimental import pallas as pl
from jax.experimental.pallas import tpu as pltpu

F32 = jnp.float32
BF16 = jnp.bfloat16

HG_HEADS = 8
HG_DK = 128
HG_DV = 128
HG_KEY = HG_HEADS * HG_DK
HG_VAL = HG_HEADS * HG_DV
SSD_HEADS = 16
SSD_HEADDIM = 64
SSD_WIDTH = SSD_HEADS * SSD_HEADDIM
SSD_GROUPS = 2
SSD_HPG = SSD_HEADS // SSD_GROUPS
SSD_STATE = 128
SSD_CONV = 4
SSD_GROUP_WIDTH = SSD_HPG * SSD_HEADDIM
SSD_BC_WIDTH = 2 * SSD_GROUPS * SSD_STATE
NORM_EPS = 1e-6

V7X_LANES = 128
V7X_SUBLANES = 8
V7X_VMEM_BYTES = 64 * 1024 * 1024
V7X_VMEM_COMPILER_RESERVE = 6 * 1024 * 1024
V7X_VMEM_UNSCOPED = 2 * 1024 * 1024

MIX_DTYPE = BF16
MIX_CHUNK = 256
HGRN_HALF = 128
HGRN_HEADS_PER_GROUP = 4
MIX_PIECE_COLS = 256
HG_HEAD_COLS = 2 * HG_DK + 2 * HG_DV
SSD_COLS = SSD_WIDTH + SSD_WIDTH + SSD_BC_WIDTH + V7X_LANES
OUTPROJ_TM = 512
OUTPROJ_SUB = 128
FFN_TM = 1024
FFN_TF = 512
FFN_SUB = 256
FFN_DOWN_COLS = 512
FFN_X1_COLS = 256
NEG_BIG = -1e30
LOG2_E = 1.4426950408889634


def _nbytes(shape, dtype):
    n = 1
    for s in shape:
        n *= s
    return n * jnp.dtype(dtype).itemsize


def _vmem_limit(block_bytes, scratch_bytes=0, value_bytes=0):
    need = 2 * block_bytes + scratch_bytes + value_bytes + V7X_VMEM_COMPILER_RESERVE
    return int(min(need, V7X_VMEM_BYTES - V7X_VMEM_UNSCOPED))


def _sigmoid(x):
    return 1.0 / (1.0 + jnp.exp2(x * (-LOG2_E)))


def _silu(x):
    return x * _sigmoid(x)


def _rms(x):
    return x * lax.rsqrt(jnp.mean(x * x, axis=-1, keepdims=True) + NORM_EPS)


def _dot(a, b):
    return jnp.dot(a, b, preferred_element_type=F32)


def _dot_nt(a, b):
    return lax.dot_general(a, b, (((1,), (1,)), ((), ())), preferred_element_type=F32)


def _split2(x, p):
    x4 = x.reshape(x.shape[0] // (2 * p), 2, p, x.shape[1])
    return x4[:, 0], x4[:, 1]


def _merge2(lo, hi):
    return jnp.stack([lo, hi], axis=1).reshape(-1, lo.shape[-1])


def _hgrn_group(heads, lvl_ref, filler):
    g = len(heads)
    w = MIX_CHUNK
    hw = HGRN_HALF
    sub = V7X_SUBLANES
    rows = g * w
    n_half = rows // hw
    assert w == 2 * hw and hw == V7X_LANES
    shp3 = (rows // sub, sub, HG_DK)
    sub_idx = lax.broadcasted_iota(jnp.int32, (1, sub, HG_DK), 1)

    q = jnp.concatenate([_silu(h[0]) * (HG_DK ** -0.5) for h in heads], axis=0)
    f = jnp.concatenate([h[4] + (1.0 - h[4]) * _sigmoid(h[1]) for h in heads], axis=0)
    k = 1.0 - f

    def half_scores(qn, kn, level, a_blocks):
        qb, kb = qn.astype(BF16), kn.astype(BF16)
        out = []
        for c in range(n_half):
            rs = slice(c * hw, (c + 1) * hw)
            s_c = _dot_nt(qb[rs], kb[rs])
            keep = lvl_ref[...] == level
            out.append(jnp.where(keep, s_c, 0.0 if a_blocks is None else a_blocks[c]))
        return out

    a_diag = half_scores(q, k, 0, None)
    filler()
    cq = f.reshape(shp3)
    ck = jnp.ones(shp3, F32)
    r = cq
    q3, k3 = q.reshape(shp3), k.reshape(shp3)
    level = 1
    p = 1
    while p < sub:
        a_diag = half_scores((q3 * cq).reshape(rows, HG_DK), (k3 * ck).reshape(rows, HG_DK), level, a_diag)
        upper = (sub_idx & p) != 0
        down = pltpu.roll(r, p, axis=1)
        up = pltpu.roll(r, sub - p, axis=1)
        cq = cq * jnp.where(upper, down, 1.0)
        ck = ck * jnp.where(upper, 1.0, up)
        r = r * jnp.where(upper, down, up)
        filler()
        p *= 2
        level += 1
    cq = cq.reshape(rows, HG_DK)
    ck = ck.reshape(rows, HG_DK)

    a_cross = None
    while p < w:
        cq_lo, cq_hi = _split2(cq, p)
        ck_lo, ck_hi = _split2(ck, p)
        q_lo, q_hi = _split2(q, p)
        k_lo, k_hi = _split2(k, p)
        nb = rows // (2 * p)
        if p < hw:
            qu = (q_hi * cq_hi).reshape(rows // 2, HG_DK).astype(BF16)
            kn = _merge2(k_lo * ck_lo, k_hi).astype(BF16)
            new_a = []
            for c in range(n_half):
                s_c = _dot_nt(qu[c * (hw // 2):(c + 1) * (hw // 2)], kn[c * hw:(c + 1) * hw])
                a_lo, a_hi = _split2(a_diag[c], p)
                _, lv_hi = _split2(lvl_ref[...], p)
                a_hi = jnp.where(lv_hi == level, s_c.reshape(a_hi.shape), a_hi)
                new_a.append(_merge2(a_lo, a_hi))
            a_diag = new_a
        else:
            qu = (q_hi * cq_hi).astype(BF16)
            kl = (k_lo * ck_lo).astype(BF16)
            a_cross = [_dot_nt(qu[i], kl[i]) for i in range(g)]
        r4 = r.reshape(nb, 2, sub, HG_DK)
        r_lo, r_hi = r4[:, 0], r4[:, 1]
        cq_hi = (cq_hi.reshape(nb, p // sub, sub, HG_DK) * r_lo[:, None]).reshape(nb, p, HG_DK)
        ck_lo = (ck_lo.reshape(nb, p // sub, sub, HG_DK) * r_hi[:, None]).reshape(nb, p, HG_DK)
        cq = _merge2(cq_lo, cq_hi)
        ck = _merge2(ck_lo, ck_hi)
        r = r_lo * r_hi
        filler()
        p *= 2
        level += 1

    qn = (q * cq).astype(BF16)
    kn = (k * ck).astype(BF16)
    results = []
    for i, (_, _, v, gate, _, nw, st) in enumerate(heads):
        rs = slice(i * w, (i + 1) * w)
        vb = v.astype(BF16)
        o_inter = _dot_nt(qn[rs], st.astype(BF16))
        o0 = _dot(a_diag[2 * i].astype(BF16), vb[:hw])
        o1 = _dot(jnp.concatenate([a_cross[i], a_diag[2 * i + 1]], axis=1).astype(BF16), vb)
        o = jnp.concatenate([o0, o1], axis=0) + o_inter
        st_new = st * r[i, 0:1, :] + _dot(v.T.astype(BF16), kn[rs])
        results.append(((_rms(o) * nw * _silu(gate)).astype(MIX_DTYPE), st_new))
    return results


def _split3(x):
    hi = x.astype(BF16)
    r1 = x - hi.astype(F32)
    mid = r1.astype(BF16)
    lo = (r1 - mid.astype(F32)).astype(BF16)
    return hi, mid, lo


def _dot_exact01_left(m01, x):
    hi, mid, lo = _split3(x)
    return _dot(m01, hi) + _dot(m01, mid) + _dot(m01, lo)


def _ssd_chunk(ps, cw_ref, cb_ref, dtb_ref, alog_ref, dskip_ref, nw_ref, o_ref, xbuf_ref, state_ref,
               filler):
    wc = MIX_CHUNK
    hdr = V7X_SUBLANES
    gw = SSD_GROUP_WIDTH
    z = ps[:, :SSD_WIDTH]
    xbc = ps[:, SSD_WIDTH:SSD_WIDTH + SSD_WIDTH + SSD_BC_WIDTH]
    dt_raw = ps[:, SSD_WIDTH + SSD_WIDTH + SSD_BC_WIDTH:]

    n_slabs = xbc.shape[1] // V7X_LANES
    cols = []
    for si in range(n_slabs):
        cs = slice(si * V7X_LANES, (si + 1) * V7X_LANES)
        xbuf_ref[si, hdr:hdr + wc, :] = xbc[:, cs]
        acc = cb_ref[:, cs]
        for kk in range(SSD_CONV):
            off = hdr - (SSD_CONV - 1) + kk
            acc = acc + cw_ref[kk:kk + 1, cs] * xbuf_ref[si, off:off + wc, :]
        xbuf_ref[si, 0:hdr, :] = xbuf_ref[si, wc:wc + hdr, :]
        cols.append(_silu(acc))
    per_group = gw // V7X_LANES
    xs = [jnp.concatenate(cols[g * per_group:(g + 1) * per_group], axis=1) for g in range(SSD_GROUPS)]
    bcs = jnp.concatenate(cols[SSD_GROUPS * per_group:], axis=1)
    filler()

    dtr = dt_raw + dtb_ref[...]
    dt = jnp.maximum(dtr, 0.0) + jnp.log1p(jnp.exp(-jnp.abs(dtr)))
    a = dt * (-jnp.exp(alog_ref[...]))
    t_idx = lax.broadcasted_iota(jnp.int32, (wc, wc), 0)
    s_idx = lax.broadcasted_iota(jnp.int32, (wc, wc), 1)
    causal = t_idx >= s_idx
    tril01 = jnp.where(causal, 1.0, 0.0).astype(BF16)
    acs = _dot_exact01_left(tril01, a) * LOG2_E
    acs_t = acs.T
    filler()

    lane = lax.broadcasted_iota(jnp.int32, (wc, V7X_LANES), 1)
    first_half = lane < SSD_HEADDIM

    def expand_heads(v):
        tiles = [jnp.where(first_half, v[:, 2 * j:2 * j + 1], v[:, 2 * j + 1:2 * j + 2])
                 for j in range(SSD_HEADS // 2)]
        return jnp.concatenate(tiles, axis=1)

    acs_x = expand_heads(acs)
    dt_x = expand_heads(dt)
    filler()

    for g in range(SSD_GROUPS):
        gs = slice(g * gw, (g + 1) * gw)
        xs_g = xs[g]
        xdt = xs_g * dt_x[:, gs]
        xdt_b = xdt.astype(BF16)
        b_g = bcs[:, g * SSD_STATE:(g + 1) * SSD_STATE]
        c_g = bcs[:, (SSD_GROUPS + g) * SSD_STATE:(SSD_GROUPS + g + 1) * SSD_STATE]
        c_b = c_g.astype(BF16)
        cb = _dot_nt(c_b, b_g.astype(BF16))

        pieces = []
        for j in range(SSD_HPG // 2):
            filler()
            xp = xdt_b[:, j * V7X_LANES:(j + 1) * V7X_LANES]
            acc = None
            for half in range(2):
                h = g * SSD_HPG + 2 * j + half
                seg = acs[:, h:h + 1] - acs_t[h:h + 1, :]
                m = (cb * jnp.exp2(jnp.where(causal, seg, NEG_BIG))).astype(BF16)
                keep = first_half if half == 0 else jnp.logical_not(first_half)
                part = _dot(m, jnp.where(keep, xp, jnp.zeros_like(xp)))
                acc = part if acc is None else acc + part
            pieces.append(acc)
        y_diag = jnp.concatenate(pieces, axis=-1)

        st = state_ref[g]
        acs_g = acs_x[:, gs]
        y_off = _dot(c_b, st.astype(BF16)) * jnp.exp2(acs_g)
        y = y_diag + y_off + dskip_ref[:, gs] * xs_g

        last = acs_g[wc - 1:wc, :]
        xdec = (xdt * jnp.exp2(last - acs_g)).astype(BF16)
        state_ref[g] = st * jnp.exp2(last) + _dot(b_g.T.astype(BF16), xdec)

        yz = y * _silu(z[:, gs])
        o_ref[:, gs] = (_rms(yz) * nw_ref[:, gs]).astype(o_ref.dtype)


def _mixer_body(x_ref, prew_ref, w_ref, lbl_ref, hnw_ref, cw_ref, cb_ref, dtb_ref, alog_ref,
                dskip_ref, snw_ref, wg_ref, wu_ref, wo_ref, oa_ref, ob_ref, wgo_ref, wuo_ref, woo_ref,
                hst_ref, lvl_ref, lb_ref, xbuf_ref, sst_ref, pj_ref, wdt_ref, *, layer, chunks_per_seq):
    hw = HGRN_HALF
    s = pl.program_id(0)
    grp = HGRN_HEADS_PER_GROUP
    ssd_lo = HG_HEADS * HG_HEAD_COLS
    dt_lo = ssd_lo + SSD_COLS - V7X_LANES
    wgo_ref[...] = wg_ref[...].astype(wgo_ref.dtype)
    wuo_ref[...] = wu_ref[...].astype(wuo_ref.dtype)
    woo_ref[...] = wo_ref[...].astype(woo_ref.dtype)
    n_all = HG_HEADS * HG_HEAD_COLS + SSD_COLS
    n_in = w_ref.shape[1]

    @pl.when(s == 0)
    def _():
        pj_ref[...] = jnp.zeros_like(pj_ref)
        wdt_ref[...] = jnp.zeros_like(wdt_ref)
        wdt_ref[:, 0:n_in - dt_lo] = w_ref[:, dt_lo:n_in]
        t_idx = lax.broadcasted_iota(jnp.int32, (hw, hw), 0)
        s_idx = lax.broadcasted_iota(jnp.int32, (hw, hw), 1)
        lv = 32 - lax.clz(t_idx ^ s_idx)
        lvl_ref[...] = jnp.where(t_idx >= s_idx, lv, -1)
        lg = lbl_ref[...]
        e = jnp.exp(lg - jnp.max(lg, axis=0, keepdims=True))
        sm = e / jnp.sum(e, axis=0, keepdims=True)
        lb_ref[...] = jnp.sum(sm[: layer + 1], axis=0, keepdims=True)

    @pl.when(jnp.logical_or(s == 0, lax.rem(s - 1, chunks_per_seq) == 0))
    def _():
        hst_ref[...] = jnp.zeros_like(hst_ref)
        sst_ref[...] = jnp.zeros_like(sst_ref)
        xbuf_ref[:, 0:V7X_SUBLANES, :] = jnp.zeros((xbuf_ref.shape[0], V7X_SUBLANES, V7X_LANES), F32)

    hb = (_rms(x_ref[...]) * prew_ref[...]).astype(BF16)
    todo = [(lo, min(MIX_PIECE_COLS, n_all - lo)) for lo in range(0, n_all, MIX_PIECE_COLS)]
    new_proj = []

    def weight_cols(lo, width):
        blocks = []
        for j in range(lo // V7X_LANES, (lo + width) // V7X_LANES):
            src = j if j >= ssd_lo // V7X_LANES else (j % 4) * HG_HEADS + j // 4
            if src * V7X_LANES == dt_lo:
                blocks.append(wdt_ref[...])
            else:
                blocks.append(w_ref[:, src * V7X_LANES:(src + 1) * V7X_LANES])
        return blocks[0] if len(blocks) == 1 else jnp.concatenate(blocks, axis=1)

    def filler():
        if todo:
            lo, width = todo.pop(0)
            new_proj.append((lo, width, _dot(hb, weight_cols(lo, width))))

    _ssd_chunk(pj_ref[:, ssd_lo:], cw_ref, cb_ref, dtb_ref, alog_ref, dskip_ref, snw_ref, ob_ref,
               xbuf_ref, sst_ref, filler)
    for gi in range(HG_HEADS // grp):
        heads = []
        for g in range(grp):
            h = gi * grp + g
            base = h * HG_HEAD_COLS
            ks = slice(h * HG_DK, (h + 1) * HG_DK)
            vs = slice(h * HG_DV, (h + 1) * HG_DV)
            heads.append((pj_ref[:, base:base + HG_DK], pj_ref[:, base + HG_DK:base + 2 * HG_DK],
                          pj_ref[:, base + 2 * HG_DK:base + 2 * HG_DK + HG_DV],
                          pj_ref[:, base + 2 * HG_DK + HG_DV:base + HG_HEAD_COLS],
                          lb_ref[:, ks], hnw_ref[:, vs], hst_ref[h]))
        for g, (out, st_new) in enumerate(_hgrn_group(heads, lvl_ref, filler)):
            h = gi * grp + g
            hst_ref[h] = st_new
            oa_ref[:, h * HG_DV:(h + 1) * HG_DV] = out
    while todo:
        filler()

    for lo, width, val in new_proj:
        pj_ref[:, lo:lo + width] = val


def _mixer(x2, pre_w, w_all, lb_logits, hgrn_nw, conv_w, conv_b, dt_bias_p, a_log_p, d_skip_x,
           ssd_nw, w_gate, w_up, w_out, seqlen, layer):
    t, d = x2.shape
    wc = MIX_CHUNK
    n_chunks = t // wc
    f = w_gate.shape[1]
    wf_rows = d // n_chunks
    assert wf_rows * n_chunks == d and wf_rows % (2 * V7X_SUBLANES) == 0
    assert w_out.shape[0] == d
    n_all = HG_HEADS * HG_HEAD_COLS + SSD_COLS
    assert w_all.shape[1] == n_all - V7X_LANES + SSD_HEADS
    xbuf_shape = ((SSD_WIDTH + SSD_BC_WIDTH) // V7X_LANES, V7X_SUBLANES + wc, V7X_LANES)
    blocks = (_nbytes((wc, d), F32) + _nbytes((wc, HG_VAL), MIX_DTYPE) + _nbytes((wc, SSD_WIDTH), MIX_DTYPE)
              + 2 * _nbytes((wf_rows, f), F32) + 2 * _nbytes((wf_rows, f), BF16)
              + _nbytes((wf_rows, d), F32) + _nbytes((wf_rows, d), BF16))
    resident = _nbytes(w_all.shape, BF16)
    scratch = (_nbytes((HG_HEADS, HG_DV, HG_DK), F32) + _nbytes((HGRN_HALF, HGRN_HALF), jnp.int32)
               + _nbytes((V7X_SUBLANES, HG_KEY), F32) + _nbytes(xbuf_shape, F32)
               + _nbytes((SSD_GROUPS, SSD_STATE, SSD_GROUP_WIDTH), F32) + _nbytes((wc, n_all), F32))
    values = _nbytes((wc, d), BF16) + _nbytes((wc, n_all), F32) + 8 * _nbytes((wc, SSD_WIDTH), F32)

    def whole(arr, **kw):
        return pl.BlockSpec(arr.shape, lambda s: (0,) * arr.ndim, **kw)

    def finished(width):
        return pl.BlockSpec((wc, width), lambda s: (jnp.maximum(s - 1, 0), 0))

    wf_spec = pl.BlockSpec((wf_rows, f), lambda s: (jnp.minimum(s, n_chunks - 1), 0))
    wo_spec = pl.BlockSpec((wf_rows, w_out.shape[1]), lambda s: (jnp.minimum(s, n_chunks - 1), 0))

    return pl.pallas_call(
        functools.partial(_mixer_body, layer=layer, chunks_per_seq=seqlen // wc),
        grid=(n_chunks + 1,),
        in_specs=[
            pl.BlockSpec((wc, d), lambda s: (jnp.minimum(s, n_chunks - 1), 0)),
            whole(pre_w),
            whole(w_all, pipeline_mode=pl.Buffered(1)),
            whole(lb_logits), whole(hgrn_nw), whole(conv_w), whole(conv_b), whole(dt_bias_p),
            whole(a_log_p), whole(d_skip_x), whole(ssd_nw), wf_spec, wf_spec, wo_spec,
        ],
        out_specs=[finished(HG_VAL), finished(SSD_WIDTH), wf_spec, wf_spec, wo_spec],
        out_shape=[jax.ShapeDtypeStruct((t, HG_VAL), MIX_DTYPE),
                   jax.ShapeDtypeStruct((t, SSD_WIDTH), MIX_DTYPE),
                   jax.ShapeDtypeStruct(w_gate.shape, BF16), jax.ShapeDtypeStruct(w_up.shape, BF16),
                   jax.ShapeDtypeStruct(w_out.shape, BF16)],
        scratch_shapes=[
            pltpu.VMEM((HG_HEADS, HG_DV, HG_DK), F32),
            pltpu.VMEM((HGRN_HALF, HGRN_HALF), jnp.int32),
            pltpu.VMEM((1, HG_KEY), F32),
            pltpu.VMEM(xbuf_shape, F32),
            pltpu.VMEM((SSD_GROUPS, SSD_STATE, SSD_GROUP_WIDTH), F32),
            pltpu.VMEM((wc, n_all), F32),
            pltpu.VMEM((d, V7X_LANES), BF16),
        ],
        compiler_params=pltpu.CompilerParams(
            dimension_semantics=("arbitrary",),
            vmem_limit_bytes=_vmem_limit(blocks, resident + scratch, values),
        ),
        name="mixer",
    )(x2, pre_w, w_all, lb_logits, hgrn_nw, conv_w, conv_b, dt_bias_p, a_log_p, d_skip_x, ssd_nw,
      w_gate, w_up, w_out)


def _outproj_body(oa_ref, ob_ref, wa_ref, wb_ref, x_ref, postw_ref, prew_ref, wd_ref,
                  x1_ref, h2_ref, wdo_ref):
    wdo_ref[...] = wd_ref[...].astype(wdo_ref.dtype)
    for r in range(x_ref.shape[0] // OUTPROJ_SUB):
        rs = slice(r * OUTPROJ_SUB, (r + 1) * OUTPROJ_SUB)
        mix = _dot(oa_ref[rs, :], wa_ref[...]) + _dot(ob_ref[rs, :], wb_ref[...])
        x1 = x_ref[rs, :] + _rms(mix) * postw_ref[...]
        x1_ref[rs, :] = x1
        h2_ref[rs, :] = (_rms(x1) * prew_ref[...]).astype(h2_ref.dtype)


def _out_proj(o_a, o_b, w_out, x2, post_w, pre_w, w_down):
    t, d = x2.shape
    tm = OUTPROJ_TM
    w_half = (w_out.shape[0] // 2, w_out.shape[1])
    assert w_half[0] == o_a.shape[1] == o_b.shape[1]
    wd_rows = w_down.shape[0] // (t // tm)
    assert wd_rows * (t // tm) == w_down.shape[0] and wd_rows % (2 * V7X_SUBLANES) == 0
    blocks = (_nbytes((tm, HG_VAL), MIX_DTYPE) + _nbytes((tm, SSD_WIDTH), MIX_DTYPE)
              + 2 * _nbytes(w_half, BF16) + 2 * _nbytes((tm, d), F32)
              + _nbytes((tm, d), BF16) + _nbytes((wd_rows, d), F32) + _nbytes((wd_rows, d), BF16))

    def row(i):
        return (i, 0)

    def fixed(i):
        return (0, 0)

    return pl.pallas_call(
        _outproj_body,
        grid=(t // tm,),
        in_specs=[
            pl.BlockSpec((tm, HG_VAL), row),
            pl.BlockSpec((tm, SSD_WIDTH), row),
            pl.BlockSpec(w_half, fixed),
            pl.BlockSpec(w_half, lambda i: (1, 0)),
            pl.BlockSpec((tm, d), row),
            pl.BlockSpec((1, d), fixed),
            pl.BlockSpec((1, d), fixed),
            pl.BlockSpec((wd_rows, d), row),
        ],
        out_specs=[pl.BlockSpec((tm, d), row), pl.BlockSpec((tm, d), row),
                   pl.BlockSpec((wd_rows, d), row)],
        out_shape=[jax.ShapeDtypeStruct((t, d), F32), jax.ShapeDtypeStruct((t, d), BF16),
                   jax.ShapeDtypeStruct(w_down.shape, BF16)],
        compiler_params=pltpu.CompilerParams(
            dimension_semantics=("parallel",),
            vmem_limit_bytes=_vmem_limit(blocks),
        ),
        name="out_proj",
    )(o_a, o_b, w_out, w_out, x2, post_w, pre_w, w_down)


def _ffn_body(h_ref, x1c_ref, wg_ref, wu_ref, wd_ref, postw_ref, o_ref, x1_ref):
    j = pl.program_id(1)
    d = o_ref.shape[1]
    xc = x1c_ref.shape[1]

    n_j = pl.num_programs(1)

    @pl.when(j < d // xc)
    def _():
        x1_ref[:, pl.ds(pl.multiple_of(j * xc, xc), xc)] = x1c_ref[...]

    def step(first, last):
        h = h_ref[...]
        hids = []
        for s in range(FFN_TF // FFN_SUB):
            cs = slice(s * FFN_SUB, (s + 1) * FFN_SUB)
            g = _dot(h, wg_ref[:, cs])
            u = _dot(h, wu_ref[:, cs])
            hids.append((_silu(g) * u).astype(BF16))
        hid = jnp.concatenate(hids, axis=1)
        ssq = None
        for c in range(d // FFN_DOWN_COLS):
            cs = slice(c * FFN_DOWN_COLS, (c + 1) * FFN_DOWN_COLS)
            part = _dot(hid, wd_ref[:, cs])
            acc = part if first else o_ref[:, cs] + part
            o_ref[:, cs] = acc
            if last:
                sq = jnp.sum(acc * acc, axis=-1, keepdims=True)
                ssq = sq if ssq is None else ssq + sq
        if last:
            scale = lax.rsqrt(ssq * (1.0 / d) + NORM_EPS)
            o_ref[...] = x1_ref[...] + o_ref[...] * scale * postw_ref[...]

    @pl.when(j == 0)
    def _():
        step(True, False)

    @pl.when(jnp.logical_and(j > 0, j < n_j - 1))
    def _():
        step(False, False)

    @pl.when(j == n_j - 1)
    def _():
        step(False, True)


def _ffn(h2, x1, w_gate, w_up, w_down, post_w):
    t, d = x1.shape
    f = w_gate.shape[1]
    tm, tf, xc = FFN_TM, FFN_TF, FFN_X1_COLS
    n_xc = d // xc
    assert f // tf >= n_xc
    blocks = (_nbytes((tm, d), BF16) + _nbytes((tm, xc), F32) + 2 * _nbytes((d, tf), BF16)
              + _nbytes((tf, d), BF16) + _nbytes((tm, d), F32))
    values = (2 * _nbytes((tm, tf), F32) + _nbytes((tm, tf), BF16) + _nbytes((tm, d), F32))
    return pl.pallas_call(
        _ffn_body,
        grid=(t // tm, f // tf),
        in_specs=[
            pl.BlockSpec((tm, d), lambda i, j: (i, 0)),
            pl.BlockSpec((tm, xc), lambda i, j: (i, jnp.minimum(j, n_xc - 1))),
            pl.BlockSpec((d, tf), lambda i, j: (0, j)),
            pl.BlockSpec((d, tf), lambda i, j: (0, j)),
            pl.BlockSpec((tf, d), lambda i, j: (j, 0)),
            pl.BlockSpec((1, d), lambda i, j: (0, 0)),
        ],
        out_specs=pl.BlockSpec((tm, d), lambda i, j: (i, 0)),
        out_shape=jax.ShapeDtypeStruct((t, d), F32),
        scratch_shapes=[pltpu.VMEM((tm, d), F32)],
        compiler_params=pltpu.CompilerParams(
            dimension_semantics=("parallel", "arbitrary"),
            vmem_limit_bytes=_vmem_limit(blocks, _nbytes((tm, d), F32), values),
        ),
        name="ffn",
    )(h2, x1, w_gate, w_up, w_down, post_w)


def _pad_lanes(v):
    return jnp.pad(v.astype(F32), (0, V7X_LANES - v.shape[0]))[None, :]


def kernel(x, pre_mix_norm_w, w_in, lb_logits, conv_w, conv_b, dt_bias, a_log, d_skip, hgrn_norm_w,
           ssd_norm_w, w_out, post_mix_norm_w, pre_ffn_norm_w, w_gate, w_up, w_down, post_ffn_norm_w):
    batch, seqlen, d = x.shape
    depth = w_in.shape[0]

    x2 = x.reshape(batch * seqlen, d)
    for l in range(depth):
        o_a, o_b, wg_b, wu_b, wo_b = _mixer(
            x2, pre_mix_norm_w[l][None, :], w_in[l].astype(BF16), lb_logits.astype(F32),
            hgrn_norm_w[l][None, :], conv_w[l], conv_b[l][None, :], _pad_lanes(dt_bias[l]),
            _pad_lanes(a_log[l]), jnp.repeat(d_skip[l].astype(F32), SSD_HEADDIM)[None, :],
            ssd_norm_w[l][None, :], w_gate[l], w_up[l], w_out[l], seqlen, l)
        x1, h2, wd_b = _out_proj(o_a, o_b, wo_b, x2, post_mix_norm_w[l][None, :],
                                 pre_ffn_norm_w[l][None, :], w_down[l])
        x2 = _ffn(h2, x1, wg_b, wu_b, wd_b, post_ffn_norm_w[l][None, :])
    return x2.reshape(batch, seqlen, d)
```

```python
import functools

import jax
import jax.numpy as jnp
from jax import lax
from jax.experimental import pallas as pl
from jax.experimental.pallas import tpu as pltpu

F32 = jnp.float32
BF16 = jnp.bfloat16

HG_HEADS = 8
HG_DK = 128
HG_DV = 128
HG_KEY = HG_HEADS * HG_DK
HG_VAL = HG_HEADS * HG_DV
SSD_HEADS = 16
SSD_HEADDIM = 64
SSD_WIDTH = SSD_HEADS * SSD_HEADDIM
SSD_GROUPS = 2
SSD_HPG = SSD_HEADS // SSD_GROUPS
SSD_STATE = 128
SSD_CONV = 4
SSD_GROUP_WIDTH = SSD_HPG * SSD_HEADDIM
SSD_BC_WIDTH = 2 * SSD_GROUPS * SSD_STATE
NORM_EPS = 1e-6

V7X_LANES = 128
V7X_SUBLANES = 8
V7X_VMEM_BYTES = 64 * 1024 * 1024
V7X_VMEM_COMPILER_RESERVE = 6 * 1024 * 1024
V7X_VMEM_UNSCOPED = 2 * 1024 * 1024

MIX_DTYPE = BF16
MIX_CHUNK = 256
HGRN_HALF = 128
HGRN_HEADS_PER_GROUP = 4
MIX_PIECE_COLS = 256
HG_HEAD_COLS = 2 * HG_DK + 2 * HG_DV
SSD_COLS = SSD_WIDTH + SSD_WIDTH + SSD_BC_WIDTH + V7X_LANES
OUTPROJ_TM = 512
OUTPROJ_SUB = 128
FFN_TM = 1024
FFN_TF = 512
FFN_SUB = 256
FFN_DOWN_COLS = 512
FFN_X1_COLS = 256
NEG_BIG = -1e30
LOG2_E = 1.4426950408889634


def _nbytes(shape, dtype):
    n = 1
    for s in shape:
        n *= s
    return n * jnp.dtype(dtype).itemsize


def _vmem_limit(block_bytes, scratch_bytes=0, value_bytes=0):
    need = 2 * block_bytes + scratch_bytes + value_bytes + V7X_VMEM_COMPILER_RESERVE
    return int(min(need, V7X_VMEM_BYTES - V7X_VMEM_UNSCOPED))


def _sigmoid(x):
    return 1.0 / (1.0 + jnp.exp2(x * (-LOG2_E)))


def _silu(x):
    return x * _sigmoid(x)


def _rms(x):
    return x * lax.rsqrt(jnp.mean(x * x, axis=-1, keepdims=True) + NORM_EPS)


def _dot(a, b):
    return jnp.dot(a, b, preferred_element_type=F32)


def _dot_nt(a, b):
    return lax.dot_general(a, b, (((1,), (1,)), ((), ())), preferred_element_type=F32)


def _split2(x, p):
    x4 = x.reshape(x.shape[0] // (2 * p), 2, p, x.shape[1])
    return x4[:, 0], x4[:, 1]


def _merge2(lo, hi):
    return jnp.stack([lo, hi], axis=1).reshape(-1, lo.shape[-1])


def _hgrn_group(heads, lvl_ref, filler):
    g = len(heads)
    w = MIX_CHUNK
    hw = HGRN_HALF
    sub = V7X_SUBLANES
    rows = g * w
    n_half = rows // hw
    assert w == 2 * hw and hw == V7X_LANES
    shp3 = (rows // sub, sub, HG_DK)
    sub_idx = lax.broadcasted_iota(jnp.int32, (1, sub, HG_DK), 1)

    q = jnp.concatenate([_silu(h[0]) * (HG_DK ** -0.5) for h in heads], axis=0)
    f = jnp.concatenate([h[4] + (1.0 - h[4]) * _sigmoid(h[1]) for h in heads], axis=0)
    k = 1.0 - f

    def half_scores(qn, kn, level, a_blocks):
        qb, kb = qn.astype(BF16), kn.astype(BF16)
        out = []
        for c in range(n_half):
            rs = slice(c * hw, (c + 1) * hw)
            s_c = _dot_nt(qb[rs], kb[rs])
            keep = lvl_ref[...] == level
            out.append(jnp.where(keep, s_c, 0.0 if a_blocks is None else a_blocks[c]))
        return out

    a_diag = half_scores(q, k, 0, None)
    filler()
    cq = f.reshape(shp3)
    ck = jnp.ones(shp3, F32)
    r = cq
    q3, k3 = q.reshape(shp3), k.reshape(shp3)
    level = 1
    p = 1
    while p < sub:
        a_diag = half_scores((q3 * cq).reshape(rows, HG_DK), (k3 * ck).reshape(rows, HG_DK), level, a_diag)
        upper = (sub_idx & p) != 0
        down = pltpu.roll(r, p, axis=1)
        up = pltpu.roll(r, sub - p, axis=1)
        cq = cq * jnp.where(upper, down, 1.0)
        ck = ck * jnp.where(upper, 1.0, up)
        r = r * jnp.where(upper, down, up)
        filler()
        p *= 2
        level += 1
    cq = cq.reshape(rows, HG_DK)
    ck = ck.reshape(rows, HG_DK)

    a_cross = None
    while p < w:
        cq_lo, cq_hi = _split2(cq, p)
        ck_lo, ck_hi = _split2(ck, p)
        q_lo, q_hi = _split2(q, p)
        k_lo, k_hi = _split2(k, p)
        nb = rows // (2 * p)
        if p < hw:
            qu = (q_hi * cq_hi).reshape(rows // 2, HG_DK).astype(BF16)
            kn = _merge2(k_lo * ck_lo, k_hi).astype(BF16)
            new_a = []
            for c in range(n_half):
                s_c = _dot_nt(qu[c * (hw // 2):(c + 1) * (hw // 2)], kn[c * hw:(c + 1) * hw])
                a_lo, a_hi = _split2(a_diag[c], p)
                _, lv_hi = _split2(lvl_ref[...], p)
                a_hi = jnp.where(lv_hi == level, s_c.reshape(a_hi.shape), a_hi)
                new_a.append(_merge2(a_lo, a_hi))
            a_diag = new_a
        else:
            qu = (q_hi * cq_hi).astype(BF16)
            kl = (k_lo * ck_lo).astype(BF16)
            a_cross = [_dot_nt(qu[i], kl[i]) for i in range(g)]
        r4 = r.reshape(nb, 2, sub, HG_DK)
        r_lo, r_hi = r4[:, 0], r4[:, 1]
        cq_hi = (cq_hi.reshape(nb, p // sub, sub, HG_DK) * r_lo[:, None]).reshape(nb, p, HG_DK)
        ck_lo = (ck_lo.reshape(nb, p // sub, sub, HG_DK) * r_hi[:, None]).reshape(nb, p, HG_DK)
        cq = _merge2(cq_lo, cq_hi)
        ck = _merge2(ck_lo, ck_hi)
        r = r_lo * r_hi
        filler()
        p *= 2
        level += 1

    qn = (q * cq).astype(BF16)
    kn = (k * ck).astype(BF16)
    results = []
    for i, (_, _, v, gate, _, nw, st) in enumerate(heads):
        rs = slice(i * w, (i + 1) * w)
        vb = v.astype(BF16)
        o_inter = _dot_nt(qn[rs], st.astype(BF16))
        o0 = _dot(a_diag[2 * i].astype(BF16), vb[:hw])
        o1 = _dot(jnp.concatenate([a_cross[i], a_diag[2 * i + 1]], axis=1).astype(BF16), vb)
        o = jnp.concatenate([o0, o1], axis=0) + o_inter
        st_new = st * r[i, 0:1, :] + _dot(v.T.astype(BF16), kn[rs])
        results.append(((_rms(o) * nw * _silu(gate)).astype(MIX_DTYPE), st_new))
    return results


def _split3(x):
    hi = x.astype(BF16)
    r1 = x - hi.astype(F32)
    mid = r1.astype(BF16)
    lo = (r1 - mid.astype(F32)).astype(BF16)
    return hi, mid, lo


def _dot_exact01_left(m01, x):
    hi, mid, lo = _split3(x)
    return _dot(m01, hi) + _dot(m01, mid) + _dot(m01, lo)


def _ssd_chunk(ps, cw_ref, cb_ref, dtb_ref, alog_ref, dskip_ref, nw_ref, o_ref, xbuf_ref, state_ref,
               filler):
    wc = MIX_CHUNK
    hdr = V7X_SUBLANES
    gw = SSD_GROUP_WIDTH
    z = ps[:, :SSD_WIDTH]
    xbc = ps[:, SSD_WIDTH:SSD_WIDTH + SSD_WIDTH + SSD_BC_WIDTH]
    dt_raw = ps[:, SSD_WIDTH + SSD_WIDTH + SSD_BC_WIDTH:]

    n_slabs = xbc.shape[1] // V7X_LANES
    cols = []
    for si in range(n_slabs):
        cs = slice(si * V7X_LANES, (si + 1) * V7X_LANES)
        xbuf_ref[si, hdr:hdr + wc, :] = xbc[:, cs]
        acc = cb_ref[:, cs]
        for kk in range(SSD_CONV):
            off = hdr - (SSD_CONV - 1) + kk
            acc = acc + cw_ref[kk:kk + 1, cs] * xbuf_ref[si, off:off + wc, :]
        xbuf_ref[si, 0:hdr, :] = xbuf_ref[si, wc:wc + hdr, :]
        cols.append(_silu(acc))
    per_group = gw // V7X_LANES
    xs = [jnp.concatenate(cols[g * per_group:(g + 1) * per_group], axis=1) for g in range(SSD_GROUPS)]
    bcs = jnp.concatenate(cols[SSD_GROUPS * per_group:], axis=1)
    filler()

    dtr = dt_raw + dtb_ref[...]
    dt = jnp.maximum(dtr, 0.0) + jnp.log1p(jnp.exp(-jnp.abs(dtr)))
    a = dt * (-jnp.exp(alog_ref[...]))
    t_idx = lax.broadcasted_iota(jnp.int32, (wc, wc), 0)
    s_idx = lax.broadcasted_iota(jnp.int32, (wc, wc), 1)
    causal = t_idx >= s_idx
    tril01 = jnp.where(causal, 1.0, 0.0).astype(BF16)
    acs = _dot_exact01_left(tril01, a) * LOG2_E
    acs_t = acs.T
    filler()

    lane = lax.broadcasted_iota(jnp.int32, (wc, V7X_LANES), 1)
    first_half = lane < SSD_HEADDIM

    def expand_heads(v):
        tiles = [jnp.where(first_half, v[:, 2 * j:2 * j + 1], v[:, 2 * j + 1:2 * j + 2])
                 for j in range(SSD_HEADS // 2)]
        return jnp.concatenate(tiles, axis=1)

    acs_x = expand_heads(acs)
    dt_x = expand_heads(dt)
    filler()

    for g in range(SSD_GROUPS):
        gs = slice(g * gw, (g + 1) * gw)
        xs_g = xs[g]
        xdt = xs_g * dt_x[:, gs]
        xdt_b = xdt.astype(BF16)
        b_g = bcs[:, g * SSD_STATE:(g + 1) * SSD_STATE]
        c_g = bcs[:, (SSD_GROUPS + g) * SSD_STATE:(SSD_GROUPS + g + 1) * SSD_STATE]
        c_b = c_g.astype(BF16)
        cb = _dot_nt(c_b, b_g.astype(BF16))

        pieces = []
        for j in range(SSD_HPG // 2):
            filler()
            xp = xdt_b[:, j * V7X_LANES:(j + 1) * V7X_LANES]
            acc = None
            for half in range(2):
                h = g * SSD_HPG + 2 * j + half
                seg = acs[:, h:h + 1] - acs_t[h:h + 1, :]
                m = (cb * jnp.exp2(jnp.where(causal, seg, NEG_BIG))).astype(BF16)
                keep = first_half if half == 0 else jnp.logical_not(first_half)
                part = _dot(m, jnp.where(keep, xp, jnp.zeros_like(xp)))
                acc = part if acc is None else acc + part
            pieces.append(acc)
        y_diag = jnp.concatenate(pieces, axis=-1)

        st = state_ref[g]
        acs_g = acs_x[:, gs]
        y_off = _dot(c_b, st.astype(BF16)) * jnp.exp2(acs_g)
        y = y_diag + y_off + dskip_ref[:, gs] * xs_g

        last = acs_g[wc - 1:wc, :]
        xdec = (xdt * jnp.exp2(last - acs_g)).astype(BF16)
        state_ref[g] = st * jnp.exp2(last) + _dot(b_g.T.astype(BF16), xdec)

        yz = y * _silu(z[:, gs])
        o_ref[:, gs] = (_rms(yz) * nw_ref[:, gs]).astype(o_ref.dtype)


def _mixer_body(x_ref, prew_ref, w_ref, lbl_ref, hnw_ref, cw_ref, cb_ref, dtb_ref, alog_ref,
                dskip_ref, snw_ref, wg_ref, wu_ref, wo_ref, oa_ref, ob_ref, wgo_ref, wuo_ref, woo_ref,
                hst_ref, lvl_ref, lb_ref, xbuf_ref, sst_ref, pj_ref, wdt_ref, *, layer, chunks_per_seq):
    hw = HGRN_HALF
    s = pl.program_id(0)
    grp = HGRN_HEADS_PER_GROUP
    ssd_lo = HG_HEADS * HG_HEAD_COLS
    dt_lo = ssd_lo + SSD_COLS - V7X_LANES
    wgo_ref[...] = wg_ref[...].astype(wgo_ref.dtype)
    wuo_ref[...] = wu_ref[...].astype(wuo_ref.dtype)
    woo_ref[...] = wo_ref[...].astype(woo_ref.dtype)
    n_all = HG_HEADS * HG_HEAD_COLS + SSD_COLS
    n_in = w_ref.shape[1]

    @pl.when(s == 0)
    def _():
        pj_ref[...] = jnp.zeros_like(pj_ref)
        wdt_ref[...] = jnp.zeros_like(wdt_ref)
        wdt_ref[:, 0:n_in - dt_lo] = w_ref[:, dt_lo:n_in]
        t_idx = lax.broadcasted_iota(jnp.int32, (hw, hw), 0)
        s_idx = lax.broadcasted_iota(jnp.int32, (hw, hw), 1)
        lv = 32 - lax.clz(t_idx ^ s_idx)
        lvl_ref[...] = jnp.where(t_idx >= s_idx, lv, -1)
        lg = lbl_ref[...]
        e = jnp.exp(lg - jnp.max(lg, axis=0, keepdims=True))
        sm = e / jnp.sum(e, axis=0, keepdims=True)
        lb_ref[...] = jnp.sum(sm[: layer + 1], axis=0, keepdims=True)

    @pl.when(jnp.logical_or(s == 0, lax.rem(s - 1, chunks_per_seq) == 0))
    def _():
        hst_ref[...] = jnp.zeros_like(hst_ref)
        sst_ref[...] = jnp.zeros_like(sst_ref)
        xbuf_ref[:, 0:V7X_SUBLANES, :] = jnp.zeros((xbuf_ref.shape[0], V7X_SUBLANES, V7X_LANES), F32)

    hb = (_rms(x_ref[...]) * prew_ref[...]).astype(BF16)
    todo = [(lo, min(MIX_PIECE_COLS, n_all - lo)) for lo in range(0, n_all, MIX_PIECE_COLS)]
    new_proj = []

    def weight_cols(lo, width):
        blocks = []
        for j in range(lo // V7X_LANES, (lo + width) // V7X_LANES):
            src = j if j >= ssd_lo // V7X_LANES else (j % 4) * HG_HEADS + j // 4
            if src * V7X_LANES == dt_lo:
                blocks.append(wdt_ref[...])
            else:
                blocks.append(w_ref[:, src * V7X_LANES:(src + 1) * V7X_LANES])
        return blocks[0] if len(blocks) == 1 else jnp.concatenate(blocks, axis=1)

    def filler():
        if todo:
            lo, width = todo.pop(0)
            new_proj.append((lo, width, _dot(hb, weight_cols(lo, width))))

    def run_heads(gi):
        heads = []
        for g in range(grp):
            h = gi * grp + g
            base = h * HG_HEAD_COLS
            ks = slice(h * HG_DK, (h + 1) * HG_DK)
            vs = slice(h * HG_DV, (h + 1) * HG_DV)
            heads.append((pj_ref[:, base:base + HG_DK], pj_ref[:, base + HG_DK:base + 2 * HG_DK],
                          pj_ref[:, base + 2 * HG_DK:base + 2 * HG_DK + HG_DV],
                          pj_ref[:, base + 2 * HG_DK + HG_DV:base + HG_HEAD_COLS],
                          lb_ref[:, ks], hnw_ref[:, vs], hst_ref[h]))
        for g, (out, st_new) in enumerate(_hgrn_group(heads, lvl_ref, filler)):
            h = gi * grp + g
            hst_ref[h] = st_new
            oa_ref[:, h * HG_DV:(h + 1) * HG_DV] = out

    n_groups = HG_HEADS // grp
    for gi in range(n_groups // 2):
        run_heads(gi)
    _ssd_chunk(pj_ref[:, ssd_lo:], cw_ref, cb_ref, dtb_ref, alog_ref, dskip_ref, snw_ref, ob_ref,
               xbuf_ref, sst_ref, filler)
    for gi in range(n_groups // 2, n_groups):
        run_heads(gi)
    while todo:
        filler()

    for lo, width, val in new_proj:
        pj_ref[:, lo:lo + width] = val


def _mixer(x2, pre_w, w_all, lb_logits, hgrn_nw, conv_w, conv_b, dt_bias_p, a_log_p, d_skip_x,
           ssd_nw, w_gate, w_up, w_out, seqlen, layer):
    t, d = x2.shape
    wc = MIX_CHUNK
    n_chunks = t // wc
    f = w_gate.shape[1]
    wf_rows = d // n_chunks
    assert wf_rows * n_chunks == d and wf_rows % (2 * V7X_SUBLANES) == 0
    assert w_out.shape[0] == d
    n_all = HG_HEADS * HG_HEAD_COLS + SSD_COLS
    assert w_all.shape[1] == n_all - V7X_LANES + SSD_HEADS
    xbuf_shape = ((SSD_WIDTH + SSD_BC_WIDTH) // V7X_LANES, V7X_SUBLANES + wc, V7X_LANES)
    blocks = (_nbytes((wc, d), F32) + _nbytes((wc, HG_VAL), MIX_DTYPE) + _nbytes((wc, SSD_WIDTH), MIX_DTYPE)
              + 2 * _nbytes((wf_rows, f), F32) + 2 * _nbytes((wf_rows, f), BF16)
              + _nbytes((wf_rows, d), F32) + _nbytes((wf_rows, d), BF16))
    resident = _nbytes(w_all.shape, BF16)
    scratch = (_nbytes((HG_HEADS, HG_DV, HG_DK), F32) + _nbytes((HGRN_HALF, HGRN_HALF), jnp.int32)
               + _nbytes((V7X_SUBLANES, HG_KEY), F32) + _nbytes(xbuf_shape, F32)
               + _nbytes((SSD_GROUPS, SSD_STATE, SSD_GROUP_WIDTH), F32) + _nbytes((wc, n_all), F32))
    values = _nbytes((wc, d), BF16) + _nbytes((wc, n_all), F32) + 8 * _nbytes((wc, SSD_WIDTH), F32)

    def whole(arr, **kw):
        return pl.BlockSpec(arr.shape, lambda s: (0,) * arr.ndim, **kw)

    def finished(width):
        return pl.BlockSpec((wc, width), lambda s: (jnp.maximum(s - 1, 0), 0))

    wf_spec = pl.BlockSpec((wf_rows, f), lambda s: (jnp.minimum(s, n_chunks - 1), 0))
    wo_spec = pl.BlockSpec((wf_rows, w_out.shape[1]), lambda s: (jnp.minimum(s, n_chunks - 1), 0))

    return pl.pallas_call(
        functools.partial(_mixer_body, layer=layer, chunks_per_seq=seqlen // wc),
        grid=(n_chunks + 1,),
        in_specs=[
            pl.BlockSpec((wc, d), lambda s: (jnp.minimum(s, n_chunks - 1), 0)),
            whole(pre_w),
            whole(w_all, pipeline_mode=pl.Buffered(1)),
            whole(lb_logits), whole(hgrn_nw), whole(conv_w), whole(conv_b), whole(dt_bias_p),
            whole(a_log_p), whole(d_skip_x), whole(ssd_nw), wf_spec, wf_spec, wo_spec,
        ],
        out_specs=[finished(HG_VAL), finished(SSD_WIDTH), wf_spec, wf_spec, wo_spec],
        out_shape=[jax.ShapeDtypeStruct((t, HG_VAL), MIX_DTYPE),
                   jax.ShapeDtypeStruct((t, SSD_WIDTH), MIX_DTYPE),
                   jax.ShapeDtypeStruct(w_gate.shape, BF16), jax.ShapeDtypeStruct(w_up.shape, BF16),
                   jax.ShapeDtypeStruct(w_out.shape, BF16)],
        scratch_shapes=[
            pltpu.VMEM((HG_HEADS, HG_DV, HG_DK), F32),
            pltpu.VMEM((HGRN_HALF, HGRN_HALF), jnp.int32),
            pltpu.VMEM((1, HG_KEY), F32),
            pltpu.VMEM(xbuf_shape, F32),
            pltpu.VMEM((SSD_GROUPS, SSD_STATE, SSD_GROUP_WIDTH), F32),
            pltpu.VMEM((wc, n_all), F32),
            pltpu.VMEM((d, V7X_LANES), BF16),
        ],
        compiler_params=pltpu.CompilerParams(
            dimension_semantics=("arbitrary",),
            vmem_limit_bytes=_vmem_limit(blocks, resident + scratch, values),
        ),
        name="mixer",
    )(x2, pre_w, w_all, lb_logits, hgrn_nw, conv_w, conv_b, dt_bias_p, a_log_p, d_skip_x, ssd_nw,
      w_gate, w_up, w_out)


def _outproj_body(oa_ref, ob_ref, wa_ref, wb_ref, x_ref, postw_ref, prew_ref, wd_ref,
                  x1_ref, h2_ref, wdo_ref):
    wdo_ref[...] = wd_ref[...].astype(wdo_ref.dtype)
    for r in range(x_ref.shape[0] // OUTPROJ_SUB):
        rs = slice(r * OUTPROJ_SUB, (r + 1) * OUTPROJ_SUB)
        mix = _dot(oa_ref[rs, :], wa_ref[...]) + _dot(ob_ref[rs, :], wb_ref[...])
        x1 = x_ref[rs, :] + _rms(mix) * postw_ref[...]
        x1_ref[rs, :] = x1
        h2_ref[rs, :] = (_rms(x1) * prew_ref[...]).astype(h2_ref.dtype)


def _out_proj(o_a, o_b, w_out, x2, post_w, pre_w, w_down):
    t, d = x2.shape
    tm = OUTPROJ_TM
    w_half = (w_out.shape[0] // 2, w_out.shape[1])
    assert w_half[0] == o_a.shape[1] == o_b.shape[1]
    wd_rows = w_down.shape[0] // (t // tm)
    assert wd_rows * (t // tm) == w_down.shape[0] and wd_rows % (2 * V7X_SUBLANES) == 0
    blocks = (_nbytes((tm, HG_VAL), MIX_DTYPE) + _nbytes((tm, SSD_WIDTH), MIX_DTYPE)
              + 2 * _nbytes(w_half, BF16) + 2 * _nbytes((tm, d), F32)
              + _nbytes((tm, d), BF16) + _nbytes((wd_rows, d), F32) + _nbytes((wd_rows, d), BF16))

    def row(i):
        return (i, 0)

    def fixed(i):
        return (0, 0)

    return pl.pallas_call(
        _outproj_body,
        grid=(t // tm,),
        in_specs=[
            pl.BlockSpec((tm, HG_VAL), row),
            pl.BlockSpec((tm, SSD_WIDTH), row),
            pl.BlockSpec(w_half, fixed),
            pl.BlockSpec(w_half, lambda i: (1, 0)),
            pl.BlockSpec((tm, d), row),
            pl.BlockSpec((1, d), fixed),
            pl.BlockSpec((1, d), fixed),
            pl.BlockSpec((wd_rows, d), row),
        ],
        out_specs=[pl.BlockSpec((tm, d), row), pl.BlockSpec((tm, d), row),
                   pl.BlockSpec((wd_rows, d), row)],
        out_shape=[jax.ShapeDtypeStruct((t, d), F32), jax.ShapeDtypeStruct((t, d), BF16),
                   jax.ShapeDtypeStruct(w_down.shape, BF16)],
        compiler_params=pltpu.CompilerParams(
            dimension_semantics=("parallel",),
            vmem_limit_bytes=_vmem_limit(blocks),
        ),
        name="out_proj",
    )(o_a, o_b, w_out, w_out, x2, post_w, pre_w, w_down)


def _ffn_body(h_ref, x1c_ref, wg_ref, wu_ref, wd_ref, postw_ref, o_ref, x1_ref):
    j = pl.program_id(1)
    d = o_ref.shape[1]
    xc = x1c_ref.shape[1]

    n_j = pl.num_programs(1)

    @pl.when(j < d // xc)
    def _():
        x1_ref[:, pl.ds(pl.multiple_of(j * xc, xc), xc)] = x1c_ref[...]

    def step(first, last):
        h = h_ref[...]
        hids = []
        for s in range(FFN_TF // FFN_SUB):
            cs = slice(s * FFN_SUB, (s + 1) * FFN_SUB)
            g = _dot(h, wg_ref[:, cs])
            u = _dot(h, wu_ref[:, cs])
            hids.append((_silu(g) * u).astype(BF16))
        hid = jnp.concatenate(hids, axis=1)
        ssq = None
        for c in range(d // FFN_DOWN_COLS):
            cs = slice(c * FFN_DOWN_COLS, (c + 1) * FFN_DOWN_COLS)
            part = _dot(hid, wd_ref[:, cs])
            acc = part if first else o_ref[:, cs] + part
            o_ref[:, cs] = acc
            if last:
                sq = jnp.sum(acc * acc, axis=-1, keepdims=True)
                ssq = sq if ssq is None else ssq + sq
        if last:
            scale = lax.rsqrt(ssq * (1.0 / d) + NORM_EPS)
            o_ref[...] = x1_ref[...] + o_ref[...] * scale * postw_ref[...]

    @pl.when(j == 0)
    def _():
        step(True, False)

    @pl.when(jnp.logical_and(j > 0, j < n_j - 1))
    def _():
        step(False, False)

    @pl.when(j == n_j - 1)
    def _():
        step(False, True)


def _ffn(h2, x1, w_gate, w_up, w_down, post_w):
    t, d = x1.shape
    f = w_gate.shape[1]
    tm, tf, xc = FFN_TM, FFN_TF, FFN_X1_COLS
    n_xc = d // xc
    assert f // tf >= n_xc
    blocks = (_nbytes((tm, d), BF16) + _nbytes((tm, xc), F32) + 2 * _nbytes((d, tf), BF16)
              + _nbytes((tf, d), BF16) + _nbytes((tm, d), F32))
    values = (2 * _nbytes((tm, tf), F32) + _nbytes((tm, tf), BF16) + _nbytes((tm, d), F32))
    return pl.pallas_call(
        _ffn_body,
        grid=(t // tm, f // tf),
        in_specs=[
            pl.BlockSpec((tm, d), lambda i, j: (i, 0)),
            pl.BlockSpec((tm, xc), lambda i, j: (i, jnp.minimum(j, n_xc - 1))),
            pl.BlockSpec((d, tf), lambda i, j: (0, j)),
            pl.BlockSpec((d, tf), lambda i, j: (0, j)),
            pl.BlockSpec((tf, d), lambda i, j: (j, 0)),
            pl.BlockSpec((1, d), lambda i, j: (0, 0)),
        ],
        out_specs=pl.BlockSpec((tm, d), lambda i, j: (i, 0)),
        out_shape=jax.ShapeDtypeStruct((t, d), F32),
        scratch_shapes=[pltpu.VMEM((tm, d), F32)],
        compiler_params=pltpu.CompilerParams(
            dimension_semantics=("parallel", "arbitrary"),
            vmem_limit_bytes=_vmem_limit(blocks, _nbytes((tm, d), F32), values),
        ),
        name="ffn",
    )(h2, x1, w_gate, w_up, w_down, post_w)


def _pad_lanes(v):
    return jnp.pad(v.astype(F32), (0, V7X_LANES - v.shape[0]))[None, :]


def kernel(x, pre_mix_norm_w, w_in, lb_logits, conv_w, conv_b, dt_bias, a_log, d_skip, hgrn_norm_w,
           ssd_norm_w, w_out, post_mix_norm_w, pre_ffn_norm_w, w_gate, w_up, w_down, post_ffn_norm_w):
    batch, seqlen, d = x.shape
    depth = w_in.shape[0]

    x2 = x.reshape(batch * seqlen, d)
    for l in range(depth):
        o_a, o_b, wg_b, wu_b, wo_b = _mixer(
            x2, pre_mix_norm_w[l][None, :], w_in[l].astype(BF16), lb_logits.astype(F32),
            hgrn_norm_w[l][None, :], conv_w[l], conv_b[l][None, :], _pad_lanes(dt_bias[l]),
            _pad_lanes(a_log[l]), jnp.repeat(d_skip[l].astype(F32), SSD_HEADDIM)[None, :],
            ssd_norm_w[l][None, :], w_gate[l], w_up[l], w_out[l], seqlen, l)
        x1, h2, wd_b = _out_proj(o_a, o_b, wo_b, x2, post_mix_norm_w[l][None, :],
                                 pre_ffn_norm_w[l][None, :], w_down[l])
        x2 = _ffn(h2, x1, wg_b, wu_b, wd_b, post_ffn_norm_w[l][None, :])
    return x2.reshape(batch, seqlen, d)
```

```python
import functools

import jax
import jax.numpy as jnp
from jax import lax
from jax.experimental import pallas as pl
from jax.experimental.pallas import tpu as pltpu

F32 = jnp.float32
BF16 = jnp.bfloat16

HG_HEADS = 8
HG_DK = 128
HG_DV = 128
HG_KEY = HG_HEADS * HG_DK
HG_VAL = HG_HEADS * HG_DV
SSD_HEADS = 16
SSD_HEADDIM = 64
SSD_WIDTH = SSD_HEADS * SSD_HEADDIM
SSD_GROUPS = 2
SSD_HPG = SSD_HEADS // SSD_GROUPS
SSD_STATE = 128
SSD_CONV = 4
SSD_GROUP_WIDTH = SSD_HPG * SSD_HEADDIM
SSD_BC_WIDTH = 2 * SSD_GROUPS * SSD_STATE
NORM_EPS = 1e-6

V7X_LANES = 128
V7X_SUBLANES = 8
V7X_VMEM_BYTES = 64 * 1024 * 1024
V7X_VMEM_COMPILER_RESERVE = 6 * 1024 * 1024
V7X_VMEM_UNSCOPED = 2 * 1024 * 1024

MIX_DTYPE = BF16
MIX_CHUNK = 256
HGRN_HALF = 128
HGRN_HEADS_PER_GROUP = 4
MIX_PIECE_COLS = 256
HG_HEAD_COLS = 2 * HG_DK + 2 * HG_DV
SSD_COLS = SSD_WIDTH + SSD_WIDTH + SSD_BC_WIDTH + V7X_LANES
OUTPROJ_TM = 512
OUTPROJ_SUB = 128
FFN_TM = 1024
FFN_TF = 512
FFN_SUB = 256
FFN_DOWN_COLS = 512
FFN_X1_COLS = 256
NEG_BIG = -1e30
LOG2_E = 1.4426950408889634


def _nbytes(shape, dtype):
    n = 1
    for s in shape:
        n *= s
    return n * jnp.dtype(dtype).itemsize


def _vmem_limit(block_bytes, scratch_bytes=0, value_bytes=0):
    need = 2 * block_bytes + scratch_bytes + value_bytes + V7X_VMEM_COMPILER_RESERVE
    return int(min(need, V7X_VMEM_BYTES - V7X_VMEM_UNSCOPED))


def _sigmoid(x):
    return 1.0 / (1.0 + jnp.exp2(x * (-LOG2_E)))


def _silu(x):
    return x * _sigmoid(x)


def _rms(x):
    return x * lax.rsqrt(jnp.mean(x * x, axis=-1, keepdims=True) + NORM_EPS)


def _dot(a, b):
    return jnp.dot(a, b, preferred_element_type=F32)


def _dot_nt(a, b):
    return lax.dot_general(a, b, (((1,), (1,)), ((), ())), preferred_element_type=F32)


def _split2(x, p):
    x4 = x.reshape(x.shape[0] // (2 * p), 2, p, x.shape[1])
    return x4[:, 0], x4[:, 1]


def _merge2(lo, hi):
    return jnp.stack([lo, hi], axis=1).reshape(-1, lo.shape[-1])


def _hgrn_group(heads, lvl_ref, filler):
    g = len(heads)
    w = MIX_CHUNK
    hw = HGRN_HALF
    sub = V7X_SUBLANES
    rows = g * w
    n_half = rows // hw
    assert w == 2 * hw and hw == V7X_LANES
    shp3 = (rows // sub, sub, HG_DK)
    sub_idx = lax.broadcasted_iota(jnp.int32, (1, sub, HG_DK), 1)

    q = jnp.concatenate([_silu(h[0]) * (HG_DK ** -0.5) for h in heads], axis=0)
    f = jnp.concatenate([h[4] + (1.0 - h[4]) * _sigmoid(h[1]) for h in heads], axis=0)
    k = 1.0 - f

    def half_scores(qn, kn, level, a_blocks):
        qb, kb = qn.astype(BF16), kn.astype(BF16)
        out = []
        for c in range(n_half):
            rs = slice(c * hw, (c + 1) * hw)
            s_c = _dot_nt(qb[rs], kb[rs])
            keep = lvl_ref[...] == level
            out.append(jnp.where(keep, s_c, 0.0 if a_blocks is None else a_blocks[c]))
        return out

    a_diag = half_scores(q, k, 0, None)
    filler()
    cq = f.reshape(shp3)
    ck = jnp.ones(shp3, F32)
    r = cq
    q3, k3 = q.reshape(shp3), k.reshape(shp3)
    level = 1
    p = 1
    while p < sub:
        a_diag = half_scores((q3 * cq).reshape(rows, HG_DK), (k3 * ck).reshape(rows, HG_DK), level, a_diag)
        upper = (sub_idx & p) != 0
        down = pltpu.roll(r, p, axis=1)
        up = pltpu.roll(r, sub - p, axis=1)
        cq = cq * jnp.where(upper, down, 1.0)
        ck = ck * jnp.where(upper, 1.0, up)
        r = r * jnp.where(upper, down, up)
        filler()
        p *= 2
        level += 1
    cq = cq.reshape(rows, HG_DK)
    ck = ck.reshape(rows, HG_DK)

    a_cross = None
    while p < w:
        cq_lo, cq_hi = _split2(cq, p)
        ck_lo, ck_hi = _split2(ck, p)
        q_lo, q_hi = _split2(q, p)
        k_lo, k_hi = _split2(k, p)
        nb = rows // (2 * p)
        if p < hw:
            qu = (q_hi * cq_hi).reshape(rows // 2, HG_DK).astype(BF16)
            kn = _merge2(k_lo * ck_lo, k_hi).astype(BF16)
            new_a = []
            for c in range(n_half):
                s_c = _dot_nt(qu[c * (hw // 2):(c + 1) * (hw // 2)], kn[c * hw:(c + 1) * hw])
                a_lo, a_hi = _split2(a_diag[c], p)
                _, lv_hi = _split2(lvl_ref[...], p)
                a_hi = jnp.where(lv_hi == level, s_c.reshape(a_hi.shape), a_hi)
                new_a.append(_merge2(a_lo, a_hi))
            a_diag = new_a
        else:
            qu = (q_hi * cq_hi).astype(BF16)
            kl = (k_lo * ck_lo).astype(BF16)
            a_cross = [_dot_nt(qu[i], kl[i]) for i in range(g)]
        r4 = r.reshape(nb, 2, sub, HG_DK)
        r_lo, r_hi = r4[:, 0], r4[:, 1]
        cq_hi = (cq_hi.reshape(nb, p // sub, sub, HG_DK) * r_lo[:, None]).reshape(nb, p, HG_DK)
        ck_lo = (ck_lo.reshape(nb, p // sub, sub, HG_DK) * r_hi[:, None]).reshape(nb, p, HG_DK)
        cq = _merge2(cq_lo, cq_hi)
        ck = _merge2(ck_lo, ck_hi)
        r = r_lo * r_hi
        filler()
        p *= 2
        level += 1

    qn = (q * cq).astype(BF16)
    kn = (k * ck).astype(BF16)
    results = []
    for i, (_, _, v, gate, _, nw, st) in enumerate(heads):
        rs = slice(i * w, (i + 1) * w)
        vb = v.astype(BF16)
        o_inter = _dot_nt(qn[rs], st.astype(BF16))
        o0 = _dot(a_diag[2 * i].astype(BF16), vb[:hw])
        o1 = _dot(jnp.concatenate([a_cross[i], a_diag[2 * i + 1]], axis=1).astype(BF16), vb)
        o = jnp.concatenate([o0, o1], axis=0) + o_inter
        st_new = st * r[i, 0:1, :] + _dot(v.T.astype(BF16), kn[rs])
        results.append(((_rms(o) * nw * _silu(gate)).astype(MIX_DTYPE), st_new))
    return results


def _split3(x):
    hi = x.astype(BF16)
    r1 = x - hi.astype(F32)
    mid = r1.astype(BF16)
    lo = (r1 - mid.astype(F32)).astype(BF16)
    return hi, mid, lo


def _dot_exact01_left(m01, x):
    hi, mid, lo = _split3(x)
    return _dot(m01, hi) + _dot(m01, mid) + _dot(m01, lo)


def _ssd_chunk(ps, cw_ref, cb_ref, dtb_ref, alog_ref, dskip_ref, nw_ref, o_ref, xbuf_ref, state_ref,
               filler):
    wc = MIX_CHUNK
    hdr = V7X_SUBLANES
    gw = SSD_GROUP_WIDTH
    z = ps[:, :SSD_WIDTH]
    xbc = ps[:, SSD_WIDTH:SSD_WIDTH + SSD_WIDTH + SSD_BC_WIDTH]
    dt_raw = ps[:, SSD_WIDTH + SSD_WIDTH + SSD_BC_WIDTH:]

    n_slabs = xbc.shape[1] // V7X_LANES
    cols = []
    for si in range(n_slabs):
        cs = slice(si * V7X_LANES, (si + 1) * V7X_LANES)
        xbuf_ref[si, hdr:hdr + wc, :] = xbc[:, cs]
        acc = cb_ref[:, cs]
        for kk in range(SSD_CONV):
            off = hdr - (SSD_CONV - 1) + kk
            acc = acc + cw_ref[kk:kk + 1, cs] * xbuf_ref[si, off:off + wc, :]
        xbuf_ref[si, 0:hdr, :] = xbuf_ref[si, wc:wc + hdr, :]
        cols.append(_silu(acc))
    per_group = gw // V7X_LANES
    xs = [jnp.concatenate(cols[g * per_group:(g + 1) * per_group], axis=1) for g in range(SSD_GROUPS)]
    bcs = jnp.concatenate(cols[SSD_GROUPS * per_group:], axis=1)
    filler()

    dtr = dt_raw + dtb_ref[...]
    dt = jnp.maximum(dtr, 0.0) + jnp.log1p(jnp.exp(-jnp.abs(dtr)))
    a = dt * (-jnp.exp(alog_ref[...]))
    t_idx = lax.broadcasted_iota(jnp.int32, (wc, wc), 0)
    s_idx = lax.broadcasted_iota(jnp.int32, (wc, wc), 1)
    causal = t_idx >= s_idx
    tril01 = jnp.where(causal, 1.0, 0.0).astype(BF16)
    acs = _dot_exact01_left(tril01, a) * LOG2_E
    acs_t = acs.T
    filler()

    lane = lax.broadcasted_iota(jnp.int32, (wc, V7X_LANES), 1)
    first_half = lane < SSD_HEADDIM

    def expand_heads(v):
        tiles = [jnp.where(first_half, v[:, 2 * j:2 * j + 1], v[:, 2 * j + 1:2 * j + 2])
                 for j in range(SSD_HEADS // 2)]
        return jnp.concatenate(tiles, axis=1)

    acs_x = expand_heads(acs)
    dt_x = expand_heads(dt)
    filler()

    for g in range(SSD_GROUPS):
        gs = slice(g * gw, (g + 1) * gw)
        xs_g = xs[g]
        xdt = xs_g * dt_x[:, gs]
        xdt_b = xdt.astype(BF16)
        b_g = bcs[:, g * SSD_STATE:(g + 1) * SSD_STATE]
        c_g = bcs[:, (SSD_GROUPS + g) * SSD_STATE:(SSD_GROUPS + g + 1) * SSD_STATE]
        c_b = c_g.astype(BF16)
        cb = _dot_nt(c_b, b_g.astype(BF16))

        pieces = []
        for j in range(SSD_HPG // 2):
            filler()
            xp = xdt_b[:, j * V7X_LANES:(j + 1) * V7X_LANES]
            acc = None
            for half in range(2):
                h = g * SSD_HPG + 2 * j + half
                seg = acs[:, h:h + 1] - acs_t[h:h + 1, :]
                m = (cb * jnp.exp2(jnp.where(causal, seg, NEG_BIG))).astype(BF16)
                keep = first_half if half == 0 else jnp.logical_not(first_half)
                part = _dot(m, jnp.where(keep, xp, jnp.zeros_like(xp)))
                acc = part if acc is None else acc + part
            pieces.append(acc)
        y_diag = jnp.concatenate(pieces, axis=-1)

        st = state_ref[g]
        acs_g = acs_x[:, gs]
        y_off = _dot(c_b, st.astype(BF16)) * jnp.exp2(acs_g)
        y = y_diag + y_off + dskip_ref[:, gs] * xs_g

        last = acs_g[wc - 1:wc, :]
        xdec = (xdt * jnp.exp2(last - acs_g)).astype(BF16)
        state_ref[g] = st * jnp.exp2(last) + _dot(b_g.T.astype(BF16), xdec)

        yz = y * _silu(z[:, gs])
        o_ref[:, gs] = (_rms(yz) * nw_ref[:, gs]).astype(o_ref.dtype)


def _mixer_body(x_ref, prew_ref, w_ref, lbl_ref, hnw_ref, cw_ref, cb_ref, dtb_ref, alog_ref,
                dskip_ref, snw_ref, wg_ref, wu_ref, wo_ref, oa_ref, ob_ref, wgo_ref, wuo_ref, woo_ref,
                hst_ref, lvl_ref, lb_ref, xbuf_ref, sst_ref, pj_ref, wdt_ref, *, layer, chunks_per_seq):
    hw = HGRN_HALF
    s = pl.program_id(0)
    grp = HGRN_HEADS_PER_GROUP
    ssd_lo = HG_HEADS * HG_HEAD_COLS
    dt_lo = ssd_lo + SSD_COLS - V7X_LANES
    wgo_ref[...] = wg_ref[...].astype(wgo_ref.dtype)
    wuo_ref[...] = wu_ref[...].astype(wuo_ref.dtype)
    woo_ref[...] = wo_ref[...].astype(woo_ref.dtype)
    n_all = HG_HEADS * HG_HEAD_COLS + SSD_COLS
    n_in = w_ref.shape[1]

    @pl.when(s == 0)
    def _():
        pj_ref[...] = jnp.zeros_like(pj_ref)
        wdt_ref[...] = jnp.zeros_like(wdt_ref)
        wdt_ref[:, 0:n_in - dt_lo] = w_ref[:, dt_lo:n_in]
        t_idx = lax.broadcasted_iota(jnp.int32, (hw, hw), 0)
        s_idx = lax.broadcasted_iota(jnp.int32, (hw, hw), 1)
        lv = 32 - lax.clz(t_idx ^ s_idx)
        lvl_ref[...] = jnp.where(t_idx >= s_idx, lv, -1)
        lg = lbl_ref[...]
        e = jnp.exp(lg - jnp.max(lg, axis=0, keepdims=True))
        sm = e / jnp.sum(e, axis=0, keepdims=True)
        lb_ref[...] = jnp.sum(sm[: layer + 1], axis=0, keepdims=True)

    @pl.when(jnp.logical_or(s == 0, lax.rem(s - 1, chunks_per_seq) == 0))
    def _():
        hst_ref[...] = jnp.zeros_like(hst_ref)
        sst_ref[...] = jnp.zeros_like(sst_ref)
        xbuf_ref[:, 0:V7X_SUBLANES, :] = jnp.zeros((xbuf_ref.shape[0], V7X_SUBLANES, V7X_LANES), F32)

    hb = (_rms(x_ref[...]) * prew_ref[...]).astype(BF16)
    todo = [(lo, min(MIX_PIECE_COLS, n_all - lo)) for lo in range(0, n_all, MIX_PIECE_COLS)]
    new_proj = []

    def weight_cols(lo, width):
        blocks = []
        for j in range(lo // V7X_LANES, (lo + width) // V7X_LANES):
            src = j if j >= ssd_lo // V7X_LANES else (j % 4) * HG_HEADS + j // 4
            if src * V7X_LANES == dt_lo:
                blocks.append(wdt_ref[...])
            else:
                blocks.append(w_ref[:, src * V7X_LANES:(src + 1) * V7X_LANES])
        return blocks[0] if len(blocks) == 1 else jnp.concatenate(blocks, axis=1)

    def filler():
        if todo:
            lo, width = todo.pop(0)
            new_proj.append((lo, width, _dot(hb, weight_cols(lo, width))))

    _ssd_chunk(pj_ref[:, ssd_lo:], cw_ref, cb_ref, dtb_ref, alog_ref, dskip_ref, snw_ref, ob_ref,
               xbuf_ref, sst_ref, filler)
    for gi in range(HG_HEADS // grp):
        heads = []
        for g in range(grp):
            h = gi * grp + g
            base = h * HG_HEAD_COLS
            ks = slice(h * HG_DK, (h + 1) * HG_DK)
            vs = slice(h * HG_DV, (h + 1) * HG_DV)
            heads.append((pj_ref[:, base:base + HG_DK], pj_ref[:, base + HG_DK:base + 2 * HG_DK],
                          pj_ref[:, base + 2 * HG_DK:base + 2 * HG_DK + HG_DV],
                          pj_ref[:, base + 2 * HG_DK + HG_DV:base + HG_HEAD_COLS],
                          lb_ref[:, ks], hnw_ref[:, vs], hst_ref[h]))
        for g, (out, st_new) in enumerate(_hgrn_group(heads, lvl_ref, filler)):
            h = gi * grp + g
            hst_ref[h] = st_new
            oa_ref[:, h * HG_DV:(h + 1) * HG_DV] = out
    while todo:
        filler()

    for lo, width, val in new_proj:
        pj_ref[:, lo:lo + width] = val


def _mixer(x2, pre_w, w_all, lb_logits, hgrn_nw, conv_w, conv_b, dt_bias_p, a_log_p, d_skip_x,
           ssd_nw, w_gate, w_up, w_out, seqlen, layer):
    t, d = x2.shape
    wc = MIX_CHUNK
    n_chunks = t // wc
    f = w_gate.shape[1]
    wf_rows = d // n_chunks
    assert wf_rows * n_chunks == d and wf_rows % (2 * V7X_SUBLANES) == 0
    assert w_out.shape[0] == d
    n_all = HG_HEADS * HG_HEAD_COLS + SSD_COLS
    assert w_all.shape[1] == n_all - V7X_LANES + SSD_HEADS
    xbuf_shape = ((SSD_WIDTH + SSD_BC_WIDTH) // V7X_LANES, V7X_SUBLANES + wc, V7X_LANES)
    blocks = (_nbytes((wc, d), F32) + _nbytes((wc, HG_VAL), MIX_DTYPE) + _nbytes((wc, SSD_WIDTH), MIX_DTYPE)
              + 2 * _nbytes((wf_rows, f), F32) + 2 * _nbytes((wf_rows, f), BF16)
              + _nbytes((wf_rows, d), F32) + _nbytes((wf_rows, d), BF16))
    resident = _nbytes(w_all.shape, BF16)
    scratch = (_nbytes((HG_HEADS, HG_DV, HG_DK), F32) + _nbytes((HGRN_HALF, HGRN_HALF), jnp.int32)
               + _nbytes((V7X_SUBLANES, HG_KEY), F32) + _nbytes(xbuf_shape, F32)
               + _nbytes((SSD_GROUPS, SSD_STATE, SSD_GROUP_WIDTH), F32) + _nbytes((wc, n_all), F32))
    values = _nbytes((wc, d), BF16) + _nbytes((wc, n_all), F32) + 8 * _nbytes((wc, SSD_WIDTH), F32)

    def whole(arr, **kw):
        return pl.BlockSpec(arr.shape, lambda s: (0,) * arr.ndim, **kw)

    def finished(width):
        return pl.BlockSpec((wc, width), lambda s: (jnp.maximum(s - 1, 0), 0))

    wf_spec = pl.BlockSpec((wf_rows, f), lambda s: (jnp.minimum(s, n_chunks - 1), 0))
    wo_spec = pl.BlockSpec((wf_rows, w_out.shape[1]), lambda s: (jnp.minimum(s, n_chunks - 1), 0))

    return pl.pallas_call(
        functools.partial(_mixer_body, layer=layer, chunks_per_seq=seqlen // wc),
        grid=(n_chunks + 1,),
        in_specs=[
            pl.BlockSpec((wc, d), lambda s: (jnp.minimum(s, n_chunks - 1), 0)),
            whole(pre_w),
            whole(w_all, pipeline_mode=pl.Buffered(1)),
            whole(lb_logits), whole(hgrn_nw), whole(conv_w), whole(conv_b), whole(dt_bias_p),
            whole(a_log_p), whole(d_skip_x), whole(ssd_nw), wf_spec, wf_spec, wo_spec,
        ],
        out_specs=[finished(HG_VAL), finished(SSD_WIDTH), wf_spec, wf_spec, wo_spec],
        out_shape=[jax.ShapeDtypeStruct((t, HG_VAL), MIX_DTYPE),
                   jax.ShapeDtypeStruct((t, SSD_WIDTH), MIX_DTYPE),
                   jax.ShapeDtypeStruct(w_gate.shape, BF16), jax.ShapeDtypeStruct(w_up.shape, BF16),
                   jax.ShapeDtypeStruct(w_out.shape, BF16)],
        scratch_shapes=[
            pltpu.VMEM((HG_HEADS, HG_DV, HG_DK), F32),
            pltpu.VMEM((HGRN_HALF, HGRN_HALF), jnp.int32),
            pltpu.VMEM((1, HG_KEY), F32),
            pltpu.VMEM(xbuf_shape, F32),
            pltpu.VMEM((SSD_GROUPS, SSD_STATE, SSD_GROUP_WIDTH), F32),
            pltpu.VMEM((wc, n_all), F32),
            pltpu.VMEM((d, V7X_LANES), BF16),
        ],
        compiler_params=pltpu.CompilerParams(
            dimension_semantics=("arbitrary",),
            vmem_limit_bytes=_vmem_limit(blocks, resident + scratch, values),
        ),
        name="mixer",
    )(x2, pre_w, w_all, lb_logits, hgrn_nw, conv_w, conv_b, dt_bias_p, a_log_p, d_skip_x, ssd_nw,
      w_gate, w_up, w_out)


def _outproj_body(oa_ref, ob_ref, wa_ref, wb_ref, x_ref, postw_ref, prew_ref, wd_ref,
                  x1_ref, h2_ref, wdo_ref):
    wdo_ref[...] = wd_ref[...].astype(wdo_ref.dtype)
    for r in range(x_ref.shape[0] // OUTPROJ_SUB):
        rs = slice(r * OUTPROJ_SUB, (r + 1) * OUTPROJ_SUB)
        mix = _dot(oa_ref[rs, :], wa_ref[...]) + _dot(ob_ref[rs, :], wb_ref[...])
        x1 = x_ref[rs, :] + _rms(mix) * postw_ref[...]
        x1_ref[rs, :] = x1
        h2_ref[rs, :] = (_rms(x1) * prew_ref[...]).astype(h2_ref.dtype)


def _out_proj(o_a, o_b, w_out, x2, post_w, pre_w, w_down):
    t, d = x2.shape
    tm = OUTPROJ_TM
    w_half = (w_out.shape[0] // 2, w_out.shape[1])
    assert w_half[0] == o_a.shape[1] == o_b.shape[1]
    wd_rows = w_down.shape[0] // (t // tm)
    assert wd_rows * (t // tm) == w_down.shape[0] and wd_rows % (2 * V7X_SUBLANES) == 0
    blocks = (_nbytes((tm, HG_VAL), MIX_DTYPE) + _nbytes((tm, SSD_WIDTH), MIX_DTYPE)
              + 2 * _nbytes(w_half, BF16) + 2 * _nbytes((tm, d), F32)
              + _nbytes((tm, d), BF16) + _nbytes((wd_rows, d), F32) + _nbytes((wd_rows, d), BF16))

    def row(i):
        return (i, 0)

    def fixed(i):
        return (0, 0)

    return pl.pallas_call(
        _outproj_body,
        grid=(t // tm,),
        in_specs=[
            pl.BlockSpec((tm, HG_VAL), row),
            pl.BlockSpec((tm, SSD_WIDTH), row),
            pl.BlockSpec(w_half, fixed, pipeline_mode=pl.Buffered(1)),
            pl.BlockSpec(w_half, lambda i: (1, 0), pipeline_mode=pl.Buffered(1)),
            pl.BlockSpec((tm, d), row),
            pl.BlockSpec((1, d), fixed),
            pl.BlockSpec((1, d), fixed),
            pl.BlockSpec((wd_rows, d), row),
        ],
        out_specs=[pl.BlockSpec((tm, d), row), pl.BlockSpec((tm, d), row),
                   pl.BlockSpec((wd_rows, d), row)],
        out_shape=[jax.ShapeDtypeStruct((t, d), F32), jax.ShapeDtypeStruct((t, d), BF16),
                   jax.ShapeDtypeStruct(w_down.shape, BF16)],
        compiler_params=pltpu.CompilerParams(
            dimension_semantics=("parallel",),
            vmem_limit_bytes=_vmem_limit(blocks),
        ),
        name="out_proj",
    )(o_a, o_b, w_out, w_out, x2, post_w, pre_w, w_down)


def _ffn_body(h_ref, x1c_ref, wg_ref, wu_ref, wd_ref, postw_ref, o_ref, x1_ref):
    j = pl.program_id(1)
    d = o_ref.shape[1]
    xc = x1c_ref.shape[1]

    n_j = pl.num_programs(1)

    @pl.when(j < d // xc)
    def _():
        x1_ref[:, pl.ds(pl.multiple_of(j * xc, xc), xc)] = x1c_ref[...]

    def step(first, last):
        h = h_ref[...]
        hids = []
        for s in range(FFN_TF // FFN_SUB):
            cs = slice(s * FFN_SUB, (s + 1) * FFN_SUB)
            g = _dot(h, wg_ref[:, cs])
            u = _dot(h, wu_ref[:, cs])
            hids.append((_silu(g) * u).astype(BF16))
        hid = jnp.concatenate(hids, axis=1)
        ssq = None
        for c in range(d // FFN_DOWN_COLS):
            cs = slice(c * FFN_DOWN_COLS, (c + 1) * FFN_DOWN_COLS)
            part = _dot(hid, wd_ref[:, cs])
            acc = part if first else o_ref[:, cs] + part
            o_ref[:, cs] = acc
            if last:
                sq = jnp.sum(acc * acc, axis=-1, keepdims=True)
                ssq = sq if ssq is None else ssq + sq
        if last:
            scale = lax.rsqrt(ssq * (1.0 / d) + NORM_EPS)
            o_ref[...] = x1_ref[...] + o_ref[...] * scale * postw_ref[...]

    @pl.when(j == 0)
    def _():
        step(True, False)

    @pl.when(jnp.logical_and(j > 0, j < n_j - 1))
    def _():
        step(False, False)

    @pl.when(j == n_j - 1)
    def _():
        step(False, True)


def _ffn(h2, x1, w_gate, w_up, w_down, post_w):
    t, d = x1.shape
    f = w_gate.shape[1]
    tm, tf, xc = FFN_TM, FFN_TF, FFN_X1_COLS
    n_xc = d // xc
    assert f // tf >= n_xc
    blocks = (_nbytes((tm, d), BF16) + _nbytes((tm, xc), F32) + 2 * _nbytes((d, tf), BF16)
              + _nbytes((tf, d), BF16) + _nbytes((tm, d), F32))
    values = (2 * _nbytes((tm, tf), F32) + _nbytes((tm, tf), BF16) + _nbytes((tm, d), F32))
    return pl.pallas_call(
        _ffn_body,
        grid=(t // tm, f // tf),
        in_specs=[
            pl.BlockSpec((tm, d), lambda i, j: (i, 0)),
            pl.BlockSpec((tm, xc), lambda i, j: (i, jnp.minimum(j, n_xc - 1))),
            pl.BlockSpec((d, tf), lambda i, j: (0, j)),
            pl.BlockSpec((d, tf), lambda i, j: (0, j)),
            pl.BlockSpec((tf, d), lambda i, j: (j, 0)),
            pl.BlockSpec((1, d), lambda i, j: (0, 0)),
        ],
        out_specs=pl.BlockSpec((tm, d), lambda i, j: (i, 0)),
        out_shape=jax.ShapeDtypeStruct((t, d), F32),
        scratch_shapes=[pltpu.VMEM((tm, d), F32)],
        compiler_params=pltpu.CompilerParams(
            dimension_semantics=("parallel", "arbitrary"),
            vmem_limit_bytes=_vmem_limit(blocks, _nbytes((tm, d), F32), values),
        ),
        name="ffn",
    )(h2, x1, w_gate, w_up, w_down, post_w)


def _pad_lanes(v):
    return jnp.pad(v.astype(F32), (0, V7X_LANES - v.shape[0]))[None, :]


def kernel(x, pre_mix_norm_w, w_in, lb_logits, conv_w, conv_b, dt_bias, a_log, d_skip, hgrn_norm_w,
           ssd_norm_w, w_out, post_mix_norm_w, pre_ffn_norm_w, w_gate, w_up, w_down, post_ffn_norm_w):
    batch, seqlen, d = x.shape
    depth = w_in.shape[0]

    x2 = x.reshape(batch * seqlen, d)
    for l in range(depth):
        o_a, o_b, wg_b, wu_b, wo_b = _mixer(
            x2, pre_mix_norm_w[l][None, :], w_in[l].astype(BF16), lb_logits.astype(F32),
            hgrn_norm_w[l][None, :], conv_w[l], conv_b[l][None, :], _pad_lanes(dt_bias[l]),
            _pad_lanes(a_log[l]), jnp.repeat(d_skip[l].astype(F32), SSD_HEADDIM)[None, :],
            ssd_norm_w[l][None, :], w_gate[l], w_up[l], w_out[l], seqlen, l)
        x1, h2, wd_b = _out_proj(o_a, o_b, wo_b, x2, post_mix_norm_w[l][None, :],
                                 pre_ffn_norm_w[l][None, :], w_down[l])
        x2 = _ffn(h2, x1, wg_b, wu_b, wd_b, post_ffn_norm_w[l][None, :])
    return x2.reshape(batch, seqlen, d)
```
